```python
import math
import jax
import jax.numpy as jnp
from jax import lax
import numpy as np

D_MODEL = 1024
BATCH = 16
SEQ = 4096
DEPTH = 2

HEAD_DIM = 64
N_GROUPS = 4
GROUP_WIDTH = D_MODEL // N_GROUPS
N_HEADS_GROUP = GROUP_WIDTH // HEAD_DIM
D_MIX = N_GROUPS * GROUP_WIDTH
ATTN_SCALE = HEAD_DIM ** -0.5
KV_RANK = 128
IDX_HEADS = 8
IDX_DIM = 32
TOPK_MAX = 256
Q_BLOCK = 128
MOBA_BLOCK = 256
MOBA_TOPK = 3
MOBA_QCHUNK = 32
HGRN_CHUNK = 64
DILATED_BRANCHES = ((128, 1), (512, 4), (2048, 16))
N_BUCKETS = 32
MAX_DISTANCE = 2048
N_ATTN_HEADS = 3 * N_HEADS_GROUP
D_FF = 2816
EPS = 1e-6
SPLITS = (GROUP_WIDTH, KV_RANK, IDX_HEADS * IDX_DIM, IDX_DIM, IDX_HEADS,
          GROUP_WIDTH, GROUP_WIDTH, GROUP_WIDTH,
          GROUP_WIDTH, GROUP_WIDTH, GROUP_WIDTH, GROUP_WIDTH,
          GROUP_WIDTH, GROUP_WIDTH, GROUP_WIDTH)
D_IN = sum(SPLITS)
SPLIT_POINTS = tuple(int(p) for p in np.cumsum(SPLITS)[:-1])

kernel_name = 'hybrid_parallel_dsa_moba_hgrn2_dilated_macaron'

F32 = jnp.float32


def rmsnorm(x, gain):
    xf = x.astype(F32)
    y = xf * lax.rsqrt(jnp.mean(xf * xf, axis=-1, keepdims=True) + EPS)
    return (y * gain.astype(F32)).astype(x.dtype)


def swiglu(h, w_gate, w_up, w_down):
    return (jax.nn.silu(h @ w_gate) * (h @ w_up)) @ w_down


def t5_bucket(dist):
    max_exact = N_BUCKETS // 2
    n = jnp.maximum(dist, 0)
    nf = jnp.maximum(n, 1).astype(F32)
    large = max_exact + (jnp.log(nf / max_exact) / math.log(MAX_DISTANCE / max_exact)
                         * (N_BUCKETS - max_exact)).astype(jnp.int32)
    large = jnp.minimum(large, N_BUCKETS - 1)
    return jnp.where(n < max_exact, n, large)


def dsa_mixer(q, ckv, iq, ik, iw, ckv_gain, w_kv_up, table):
    B, T, _ = q.shape
    H, hd = N_HEADS_GROUP, HEAD_DIM
    c = rmsnorm(ckv, ckv_gain)
    w_up = w_kv_up.reshape(KV_RANK, 2, H, hd)
    w_uk, w_uv = w_up[:, 0], w_up[:, 1]
    q = q.reshape(B, T, H, hd)
    iq = iq.reshape(B, T, IDX_HEADS, IDX_DIM)
    iw = iw * (IDX_HEADS * IDX_DIM) ** -0.5
    topk = min(TOPK_MAX, T // 4)
    key_pos = jnp.arange(T)
    bidx = jnp.arange(B)[:, None, None]

    def block(n):
        t0 = n * Q_BLOCK
        qb = lax.dynamic_slice_in_dim(q, t0, Q_BLOCK, 1)
        iqb = lax.dynamic_slice_in_dim(iq, t0, Q_BLOCK, 1)
        iwb = lax.dynamic_slice_in_dim(iw, t0, Q_BLOCK, 1)
        qpos = t0 + jnp.arange(Q_BLOCK)
        rel = jax.nn.relu(jnp.einsum('bqhd,bsd->bqhs', iqb, ik))
        index = jnp.einsum('bqh,bqhs->bqs', iwb, rel).astype(F32)
        index = jnp.where(key_pos[None, None, :] <= qpos[None, :, None], index, -jnp.inf)
        _, sel = lax.top_k(index, topk)
        valid = sel <= qpos[None, :, None]
        cg = c[bidx, sel]
        q_abs = jnp.einsum('bqhd,rhd->bqhr', qb, w_uk)
        logits = jnp.einsum('bqhr,bqkr->bhqk', q_abs, cg).astype(F32) * ATTN_SCALE
        bias = table[t5_bucket(qpos[None, :, None] - sel)]
        logits = jnp.where(valid[:, None], logits + jnp.moveaxis(bias, -1, 1).astype(F32), -jnp.inf)
        p = jax.nn.softmax(logits, axis=-1).astype(c.dtype)
        ctx = jnp.einsum('bhqk,bqkr->bqhr', p, cg)
        return jnp.einsum('bqhr,rhd->bqhd', ctx, w_uv)

    out = lax.map(block, jnp.arange(T // Q_BLOCK))
    return jnp.moveaxis(out, 0, 1).reshape(B, T, H * hd)


def moba_mixer(q, k, v, table):
    B, T, _ = q.shape
    H, hd, BLK, QC = N_HEADS_GROUP, HEAD_DIM, MOBA_BLOCK, MOBA_QCHUNK
    Tp = -(-T // BLK) * BLK
    nb = Tp // BLK
    topb = min(MOBA_TOPK, nb)

    def heads(a):
        a = a.reshape(B, T, H, hd).transpose(0, 2, 1, 3)
        return jnp.pad(a, ((0, 0), (0, 0), (0, Tp - T), (0, 0)))

    qh, kh, vh = heads(q), heads(k), heads(v)
    kb = kh.reshape(B, H, nb, BLK, hd)
    vb = vh.reshape(B, H, nb, BLK, hd)
    kmean = jnp.mean(kb.astype(F32), axis=3).astype(kh.dtype)
    bi = jnp.arange(B)[:, None, None, None]
    hi = jnp.arange(H)[None, :, None, None]
    hi5 = hi[..., None]
    blk_ids = jnp.arange(nb)
    inblk = jnp.arange(BLK)

    def step(i):
        t0 = i * QC
        own = t0 // BLK
        qb = lax.dynamic_slice_in_dim(qh, t0, QC, 2)
        qpos = t0 + jnp.arange(QC)
        gate = jnp.einsum('bhqd,bhnd->bhqn', qb, kmean).astype(F32)
        gate = jnp.where(blk_ids < own, gate, -jnp.inf)
        _, sel = lax.top_k(gate, topb)
        sel_ok = sel < own
        kg = kb[bi, hi, sel]
        vg = vb[bi, hi, sel]
        l_sel = jnp.einsum('bhqd,bhqnkd->bhqnk', qb, kg).astype(F32) * ATTN_SCALE
        d_sel = qpos[:, None, None] - (sel[..., None] * BLK + inblk)
        l_sel = jnp.where(sel_ok[..., None], l_sel + table[t5_bucket(d_sel), hi5].astype(F32), -jnp.inf)
        ko = lax.dynamic_index_in_dim(kb, own, axis=2, keepdims=False)
        vo = lax.dynamic_index_in_dim(vb, own, axis=2, keepdims=False)
        l_own = jnp.einsum('bhqd,bhkd->bhqk', qb, ko).astype(F32) * ATTN_SCALE
        d_own = qpos[:, None] - (own * BLK + inblk)[None, :]
        l_own = jnp.where(d_own >= 0, l_own + table[t5_bucket(d_own)].transpose(2, 0, 1).astype(F32), -jnp.inf)
        logits = jnp.concatenate([l_sel.reshape(B, H, QC, topb * BLK), l_own], axis=-1)
        p = jax.nn.softmax(logits, axis=-1).astype(vh.dtype)
        p_sel = p[..., :topb * BLK].reshape(B, H, QC, topb, BLK)
        p_own = p[..., topb * BLK:]
        return (jnp.einsum('bhqnk,bhqnkd->bhqd', p_sel, vg)
                + jnp.einsum('bhqk,bhkd->bhqd', p_own, vo))

    out = lax.map(step, jnp.arange(Tp // QC))
    return out.transpose(1, 0, 3, 2, 4).reshape(B, Tp, H * hd)[:, :T]


def hgrn2_mixer(q, fpre, inp, g, lb, norm_gain):
    B, T, _ = q.shape
    H, dk, dv, C = N_HEADS_GROUP, HEAD_DIM, HEAD_DIM, HGRN_CHUNK
    N = T // C
    f = lb + (1.0 - lb) * jax.nn.sigmoid(fpre.astype(F32))
    logf = jnp.log(f)
    kin = 1.0 - f

    def chunks(a, d):
        return a.astype(F32).reshape(B, N, C, H, d).transpose(1, 0, 3, 2, 4)

    qc, kc, vc = chunks(q, dk), chunks(kin, dk), chunks(inp, dv)
    bc = jnp.cumsum(chunks(logf, dk), axis=3)
    causal = jnp.arange(C)[:, None] >= jnp.arange(C)[None, :]

    def step(S, xs):
        qx, kx, vx, bx = xs
        diff = bx[:, :, :, None, :] - bx[:, :, None, :, :]
        decay = jnp.exp(jnp.where(causal[:, :, None], diff, -jnp.inf))
        A = jnp.einsum('bhtk,bhsk,bhtsk->bhts', qx, kx, decay)
        o = (jnp.einsum('bhts,bhsv->bhtv', A, vx)
             + jnp.einsum('bhtk,bhkv->bhtv', qx * jnp.exp(bx), S))
        b_last = bx[:, :, -1]
        S_new = (S * jnp.exp(b_last)[..., None]
                 + jnp.einsum('bhsk,bhsv->bhkv', kx * jnp.exp(b_last[:, :, None] - bx), vx))
        return S_new, o

    S0 = jnp.zeros((B, H, dk, dv), F32)
    _, o = lax.scan(step, S0, (qc, kc, vc, bc))
    o = o.transpose(1, 0, 3, 2, 4).reshape(B, T, H, dv)
    o = o * lax.rsqrt(jnp.mean(o * o, axis=-1, keepdims=True) + EPS)
    o = o.reshape(B, T, H * dv) * norm_gain.astype(F32) * jax.nn.silu(g.astype(F32))
    return o.astype(q.dtype)


def dilated_mixer(q, k, v, table):
    B, T, _ = q.shape
    H, hd = N_HEADS_GROUP, HEAD_DIM

    def heads(a):
        return a.reshape(B, T, H, hd).transpose(0, 2, 1, 3)

    qh, kh, vh = heads(q), heads(k), heads(v)
    ms, ss, nums = [], [], []
    for window, r in DILATED_BRANCHES:
        span = window // r
        unit = r * span
        Tp = -(-T // unit) * unit
        L = Tp // r
        nblk = L // span

        def strided(a):
            a = jnp.pad(a, ((0, 0), (0, 0), (0, Tp - T), (0, 0)))
            return a.reshape(B, H, L, r, hd).transpose(0, 1, 3, 2, 4).reshape(B, H, r, nblk, span, hd)

        def with_prev(a):
            prev = jnp.pad(a, ((0, 0), (0, 0), (0, 0), (1, 0), (0, 0), (0, 0)))[:, :, :, :-1]
            return jnp.concatenate([prev, a], axis=4)

        qs = strided(qh)
        kcat, vcat = with_prev(strided(kh)), with_prev(strided(vh))
        logits = jnp.einsum('bhrnqd,bhrnkd->bhrnqk', qs, kcat).astype(F32) * ATTN_SCALE
        a_i = jnp.arange(span)[:, None]
        c_i = jnp.arange(2 * span)[None, :]
        delta = a_i + span - c_i
        key_idx = jnp.arange(nblk)[:, None, None] * span - span + c_i[None]
        mask = (delta >= 0) & (delta <= span) & (key_idx >= 0)
        bias = table[t5_bucket(delta * r)].transpose(2, 0, 1)[:, None, None].astype(F32)
        logits = jnp.where(mask, logits + bias, -jnp.inf)
        m = jnp.max(logits, axis=-1)
        p = jnp.exp(logits - m[..., None])
        s = jnp.sum(p, axis=-1)
        num = jnp.einsum('bhrnqk,bhrnkd->bhrnqd', p.astype(vh.dtype), vcat).astype(F32)
        ms.append(m.reshape(B, H, r, L).transpose(0, 1, 3, 2).reshape(B, H, Tp)[:, :, :T])
        ss.append(s.reshape(B, H, r, L).transpose(0, 1, 3, 2).reshape(B, H, Tp)[:, :, :T])
        nums.append(num.reshape(B, H, r, L, hd).transpose(0, 1, 3, 2, 4).reshape(B, H, Tp, hd)[:, :, :T])
    m_stack = jnp.stack(ms)
    w = jnp.exp(m_stack - jnp.max(m_stack, axis=0))
    den = jnp.sum(w * jnp.stack(ss), axis=0)
    out = jnp.sum(w[..., None] * jnp.stack(nums), axis=0) / den[..., None]
    return out.transpose(0, 2, 1, 3).reshape(B, T, H * hd).astype(q.dtype)


def setup_inputs(seed: int = 0) -> dict:
    key = jax.random.key(seed)
    ks = jax.random.split(key, 20)

    def nrm(k, shape, scale):
        return jax.random.normal(k, shape, F32) * scale

    def gain(k, shape):
        return 1.0 + 0.05 * jax.random.normal(k, shape, F32)

    return {
        'x': nrm(ks[0], (BATCH, SEQ, D_MODEL), 1.0),
        'norm_ffn1': gain(ks[1], (DEPTH, D_MODEL)),
        'ffn1_gate': nrm(ks[2], (DEPTH, D_MODEL, D_FF), D_MODEL ** -0.5),
        'ffn1_up': nrm(ks[3], (DEPTH, D_MODEL, D_FF), D_MODEL ** -0.5),
        'ffn1_down': nrm(ks[4], (DEPTH, D_FF, D_MODEL), D_FF ** -0.5),
        'norm_mix': gain(ks[5], (DEPTH, D_MODEL)),
        'w_in': nrm(ks[6], (DEPTH, D_MODEL, D_IN), D_MODEL ** -0.5),
        'ckv_norm': gain(ks[7], (DEPTH, KV_RANK)),
        'w_kv_up': nrm(ks[8], (DEPTH, KV_RANK, 2 * GROUP_WIDTH), KV_RANK ** -0.5),
        'hgrn_lb_logits': nrm(ks[9], (DEPTH, GROUP_WIDTH), 0.5),
        'hgrn_norm': gain(ks[10], (DEPTH, GROUP_WIDTH)),
        'w_out': nrm(ks[11], (DEPTH, D_MIX, D_MODEL), D_MIX ** -0.5),
        'norm_ffn2': gain(ks[12], (DEPTH, D_MODEL)),
        'ffn2_gate': nrm(ks[13], (DEPTH, D_MODEL, D_FF), D_MODEL ** -0.5),
        'ffn2_up': nrm(ks[14], (DEPTH, D_MODEL, D_FF), D_MODEL ** -0.5),
        'ffn2_down': nrm(ks[15], (DEPTH, D_FF, D_MODEL), D_FF ** -0.5),
        'rel_bias': nrm(ks[16], (N_BUCKETS, N_ATTN_HEADS), 0.5),
        'norm_final': gain(ks[17], (D_MODEL,)),
    }


def reference(x, norm_ffn1, ffn1_gate, ffn1_up, ffn1_down, norm_mix, w_in, ckv_norm, w_kv_up,
              hgrn_lb_logits, hgrn_norm, w_out, norm_ffn2, ffn2_gate, ffn2_up, ffn2_down,
              rel_bias, norm_final):
    H = N_HEADS_GROUP
    lb_w = jax.nn.softmax(hgrn_lb_logits.astype(F32), axis=0)
    lower_bounds = jnp.cumsum(lb_w, axis=0) - lb_w[0]
    table_a, table_b, table_d = rel_bias[:, :H], rel_bias[:, H:2 * H], rel_bias[:, 2 * H:3 * H]
    for l in range(DEPTH):
        h = rmsnorm(x, norm_ffn1[l])
        x = x + 0.5 * swiglu(h, ffn1_gate[l], ffn1_up[l], ffn1_down[l])
        h = rmsnorm(x, norm_mix[l])
        (a_q, a_ckv, a_iq, a_ik, a_iw, b_q, b_k, b_v, c_q, c_f, c_i, c_g,
         d_q, d_k, d_v) = jnp.split(h @ w_in[l], SPLIT_POINTS, axis=-1)
        o_a = dsa_mixer(a_q, a_ckv, a_iq, a_ik, a_iw, ckv_norm[l], w_kv_up[l], table_a)
        o_b = moba_mixer(b_q, b_k, b_v, table_b)
        o_c = hgrn2_mixer(c_q, c_f, c_i, c_g, lower_bounds[l], hgrn_norm[l])
        o_d = dilated_mixer(d_q, d_k, d_v, table_d)
        x = x + jnp.concatenate([o_a, o_b, o_c, o_d], axis=-1) @ w_out[l]
        h = rmsnorm(x, norm_ffn2[l])
        x = x + 0.5 * swiglu(h, ffn2_gate[l], ffn2_up[l], ffn2_down[l])
    return rmsnorm(x, norm_final)
```

```python
import functools
import math

import jax
import jax.numpy as jnp
import numpy as np
from jax import lax
from jax.experimental import pallas as pl
from jax.experimental.pallas import tpu as pltpu

F32 = jnp.float32
BF16 = jnp.bfloat16
I32 = jnp.int32

HEAD_DIM = 64
N_GROUPS = 4
GROUP_WIDTH = 256
N_HEADS_GROUP = GROUP_WIDTH // HEAD_DIM
ATTN_SCALE = HEAD_DIM ** -0.5
KV_RANK = 128
IDX_HEADS = 8
IDX_DIM = 32
TOPK_MAX = 256
MOBA_BLOCK = 256
MOBA_TOPK = 3
HGRN_CHUNK = 64
DILATED_BRANCHES = ((128, 1), (512, 4), (2048, 16))
DIL_SPAN = 128
N_BUCKETS = 32
MAX_DISTANCE = 2048
EPS = 1e-6

LANES = 128
VMEM_LIMIT_BYTES = 56 * 1024 * 1024

NEG = -1e30
INT_MIN = -2 ** 31

GA_W, GB_W, GC_W, GD_W = 768, 768, 1024, 768
A_IW_LANE = 32


def _dot(a, b):
    return jnp.dot(a, b, preferred_element_type=F32)


def _dot_nt(a, b):
    return lax.dot_general(a, b, (((1,), (1,)), ((), ())), preferred_element_type=F32)


def _dot_tn(a, b):
    return lax.dot_general(a, b, (((0,), (0,)), ((), ())), preferred_element_type=F32)


def _rms(x, gain):
    return x * lax.rsqrt(jnp.mean(x * x, axis=-1, keepdims=True) + EPS) * gain


def _split3(x):
    hi = x.astype(BF16)
    r1 = x - hi.astype(F32)
    mid = r1.astype(BF16)
    lo = (r1 - mid.astype(F32)).astype(BF16)
    return hi, mid, lo


def _params(sem):
    return pltpu.CompilerParams(dimension_semantics=sem, vmem_limit_bytes=VMEM_LIMIT_BYTES)


def _resident(shape):
    nd = len(shape)
    return pl.BlockSpec(shape, lambda *_: (0,) * nd, pipeline_mode=pl.Buffered(1))


def _bucket_np(dist):
    max_exact = N_BUCKETS // 2
    n = np.maximum(dist, 0)
    nf = np.maximum(n, 1).astype(np.float64)
    large = max_exact + (np.log(nf / max_exact) / math.log(MAX_DISTANCE / max_exact)
                         * (N_BUCKETS - max_exact)).astype(np.int64)
    large = np.minimum(large, N_BUCKETS - 1)
    return np.where(n < max_exact, n, large).astype(np.int32)


def _toeplitz_bucket_ids(seq):
    nj = seq // LANES
    q = np.arange(LANES)[:, None]
    s = np.arange(LANES)[None, :]
    j = np.arange(nj)[:, None, None]
    return _bucket_np(LANES * j + q - s)


def _dilated_bucket_ids():
    a = np.arange(DIL_SPAN)[:, None]
    c = np.arange(2 * DIL_SPAN)[None, :]
    delta = a + DIL_SPAN - c
    return np.stack([_bucket_np(delta * r) for _, r in DILATED_BRANCHES])


def _bias_tiles_body(tab_ref, ids_ref, o_ref, *, head0):
    ids = ids_ref[0]
    for h in range(N_HEADS_GROUP):
        acc = jnp.zeros(ids.shape, F32)
        for k in range(N_BUCKETS):
            acc = jnp.where(ids == k, tab_ref[k, head0 + h], acc)
        o_ref[h, 0] = acc


def _bias_tiles(rel_bias, ids, head0):
    n, r, c = ids.shape
    return pl.pallas_call(
        functools.partial(_bias_tiles_body, head0=head0),
        grid=(n,),
        in_specs=[pl.BlockSpec(memory_space=pltpu.SMEM),
                  pl.BlockSpec((1, r, c), lambda i: (i, 0, 0))],
        out_specs=pl.BlockSpec((N_HEADS_GROUP, 1, r, c), lambda i: (0, i, 0, 0)),
        out_shape=jax.ShapeDtypeStruct((N_HEADS_GROUP, n, r, c), F32),
        compiler_params=_params(("arbitrary",)),
        name="bias_tiles",
    )(rel_bias, jnp.asarray(ids))


FFN_CHUNK = 256


def _swiglu_residual(x, gain, wg_ref, wu_ref, wd_ref):
    h = _rms(x, gain).astype(BF16)
    d_ff = wg_ref.shape[1]
    acc = jnp.zeros(x.shape, F32)
    for c0 in range(0, d_ff, FFN_CHUNK):
        g = _dot(h, wg_ref[:, c0:c0 + FFN_CHUNK])
        u = _dot(h, wu_ref[:, c0:c0 + FFN_CHUNK])
        a = (g * jax.nn.sigmoid(g) * u).astype(BF16)
        acc = acc + _dot(a, wd_ref[c0:c0 + FFN_CHUNK, :])
    return x + 0.5 * acc


def _ffn_body(x_ref, g_ref, wg_ref, wu_ref, wd_ref, o_ref):
    o_ref[...] = _swiglu_residual(x_ref[...], g_ref[...], wg_ref, wu_ref, wd_ref)


def _ffn(x2d, gain, wg, wu, wd, tm):
    m, d = x2d.shape
    return pl.pallas_call(
        _ffn_body,
        grid=(m // tm,),
        in_specs=[pl.BlockSpec((tm, d), lambda i: (i, 0)),
                  _resident((1, d)), _resident(wg.shape), _resident(wu.shape), _resident(wd.shape)],
        out_specs=pl.BlockSpec((tm, d), lambda i: (i, 0)),
        out_shape=jax.ShapeDtypeStruct((m, d), F32),
        compiler_params=_params(("parallel",)),
        name="ffn",
    )(x2d, gain, wg, wu, wd)


def _inproj_body(x_ref, g_ref, w_ref, za_ref, zb_ref, zc_ref, zd_ref):
    h = _rms(x_ref[...], g_ref[...]).astype(BF16)
    c0 = 0
    for ref in (za_ref, zb_ref, zc_ref, zd_ref):
        w = ref.shape[1]
        ref[...] = _dot(h, w_ref[:, c0:c0 + w])
        c0 += w


def _inproj(x2d, gain, w_packed, tm):
    m, d = x2d.shape
    widths = (GA_W, GB_W, GC_W, GD_W)
    return pl.pallas_call(
        _inproj_body,
        grid=(m // tm,),
        in_specs=[pl.BlockSpec((tm, d), lambda i: (i, 0)), _resident((1, d)), _resident(w_packed.shape)],
        out_specs=[pl.BlockSpec((tm, w), lambda i: (i, 0)) for w in widths],
        out_shape=[jax.ShapeDtypeStruct((m, w), F32) for w in widths],
        compiler_params=_params(("parallel",)),
        name="inproj",
    )(x2d, gain, w_packed)


def _pack_w_in(w):
    d = w.shape[0]
    pad = jnp.zeros((d, GA_W - 680), w.dtype)
    return jnp.concatenate([w[:, 0:256], w[:, 384:640], w[:, 256:384], w[:, 640:680], pad, w[:, 680:]],
                           axis=1).astype(BF16)


DSA_TQ = 128


def _online_softmax_step(s, v_bf16, m, l, acc):
    m_new = jnp.maximum(m, jnp.max(s, axis=1, keepdims=True))
    alpha = jnp.exp(m - m_new)
    p = jnp.exp(s - m_new)
    l = alpha * l + jnp.sum(p, axis=1, keepdims=True)
    acc = alpha * acc + _dot(p.astype(BF16), v_bf16)
    return m_new, l, acc


def _dsa_body(q_ref, iq_ref, iwq_ref, ckv_ref, ikw_ref, gain_ref, wuk_ref, wuv_ref, bias_ref,
              o_ref, c_sc, ik_sc, key_sc, *, topk):
    tq = DSA_TQ
    n = pl.program_id(1)
    seq = ckv_ref.shape[0]
    nheads = N_HEADS_GROUP
    lane = lax.broadcasted_iota(I32, (1, LANES), 1)

    @pl.when(n == 0)
    def _():
        rows = min(seq, 512)
        for r0 in range(0, seq, rows):
            c_sc[r0:r0 + rows, :] = _rms(ckv_ref[r0:r0 + rows, :], gain_ref[...]).astype(BF16)
            ik = jnp.where(lane < IDX_DIM, ikw_ref[r0:r0 + rows, :], 0.0)
            ik4 = ik
            for rep in range(1, LANES // IDX_DIM):
                ik4 = ik4 + pltpu.roll(ik, rep * IDX_DIM, 1)
            ik_sc[r0:r0 + rows, :] = ik4.astype(BF16)

    iq = iq_ref[...]
    iw = iwq_ref[...] * (IDX_HEADS * IDX_DIM) ** -0.5
    heads_per_group = LANES // IDX_DIM
    lhs = []
    for h in range(IDX_HEADS):
        blk = iq[:, LANES * (h // heads_per_group):LANES * (h // heads_per_group + 1)]
        lo = IDX_DIM * (h % heads_per_group)
        lhs.append(jnp.where((lane >= lo) & (lane < lo + IDX_DIM), blk, 0.0).astype(BF16))
    qpos = n * tq + lax.broadcasted_iota(I32, (tq, 1), 0)
    sidx = lax.broadcasted_iota(I32, (1, tq), 1)
    nchunks = n + 1

    def score_chunk(kc, carry):
        k0 = pl.multiple_of(kc * tq, tq)
        ikc = ik_sc[pl.ds(k0, tq), :]
        idx = jnp.zeros((tq, tq), F32)
        for h in range(IDX_HEADS):
            s = _dot_nt(lhs[h], ikc)
            idx = idx + iw[:, A_IW_LANE + h:A_IW_LANE + h + 1] * jnp.maximum(s, 0.0)
        bits = pltpu.bitcast(idx, I32)
        keys = bits ^ ((bits >> 31) & 0x7FFFFFFF)
        keys = jnp.where(k0 + sidx <= qpos, keys, INT_MIN)
        key_sc[:, pl.ds(k0, tq)] = keys
        return carry

    lax.fori_loop(0, nchunks, score_chunk, 0)

    def count(pred):
        def body(kc, acc):
            k0 = pl.multiple_of(kc * tq, tq)
            return acc + jnp.where(pred(key_sc[:, pl.ds(k0, tq)], k0), 1, 0)
        acc = lax.fori_loop(0, nchunks, body, jnp.zeros((tq, tq), I32))
        return jnp.sum(acc, axis=1, keepdims=True)

    def bit_iter(i, t):
        cand = t | lax.shift_left(jnp.int32(1), 31 - i)
        cand_s = cand ^ INT_MIN
        cnt = count(lambda keys, k0: keys >= cand_s)
        return jnp.where(cnt >= topk, cand, t)

    t = lax.fori_loop(0, 32, bit_iter, jnp.zeros((tq, 1), I32))
    tau = t ^ INT_MIN
    cnt_gt = count(lambda keys, k0: keys > tau)
    cnt_eq = count(lambda keys, k0: keys == tau)
    need = topk - cnt_gt
    tie_rows = (cnt_eq > need) & (t != 0)

    @pl.when(jnp.max(tie_rows.astype(I32)) > 0)
    def _():
        def pos_iter(i, p):
            cand = p | lax.shift_left(jnp.int32(1), (seq.bit_length() - 1) - i)
            cnt = count(lambda keys, k0: (keys == tau) & (k0 + sidx < cand))
            return jnp.where(cnt < need, cand, p)
        p = lax.fori_loop(0, seq.bit_length(), pos_iter, jnp.zeros((tq, 1), I32))
        p = jnp.where(tie_rows, p, seq)

        def drop(kc, carry):
            k0 = pl.multiple_of(kc * tq, tq)
            keys = key_sc[:, pl.ds(k0, tq)]
            key_sc[:, pl.ds(k0, tq)] = jnp.where((keys == tau) & (k0 + sidx > p), INT_MIN, keys)
            return carry
        lax.fori_loop(0, nchunks, drop, 0)

    tau_sel = jnp.maximum(tau, INT_MIN + 1)

    q = q_ref[...]
    qs = []
    for h in range(nheads):
        qh = q[:, h * HEAD_DIM:(h + 1) * HEAD_DIM].astype(BF16)
        qs.append((_dot(qh, wuk_ref[h]) * ATTN_SCALE).astype(BF16))
    qs = jnp.concatenate(qs, axis=0)
    rows = nheads * tq

    def attn_chunk(kc, carry):
        m, l, acc = carry
        k0 = pl.multiple_of(kc * tq, tq)
        ck = c_sc[pl.ds(k0, tq), :]
        s = _dot_nt(qs, ck)
        bias = jnp.concatenate([bias_ref[h, n - kc] for h in range(nheads)], axis=0)
        drop = jnp.where(key_sc[:, pl.ds(k0, tq)] >= tau_sel, 0.0, NEG)
        s = s + bias + jnp.concatenate([drop] * nheads, axis=0)
        return _online_softmax_step(s, ck, m, l, acc)

    m0 = jnp.full((rows, 1), NEG, F32)
    l0 = jnp.zeros((rows, 1), F32)
    acc0 = jnp.zeros((rows, KV_RANK), F32)
    m, l, acc = lax.fori_loop(0, nchunks, attn_chunk, (m0, l0, acc0))
    ctx = (acc / l).astype(BF16)
    outs = [_dot(ctx[h * tq:(h + 1) * tq], wuv_ref[h]) for h in range(nheads)]
    o_ref[...] = jnp.concatenate(outs, axis=1)


def _dsa(za, ckv_gain, wuk_t, wuv, bias, topk):
    b, seq, _ = za.shape
    tq = DSA_TQ
    return pl.pallas_call(
        functools.partial(_dsa_body, topk=topk),
        grid=(b, seq // tq),
        in_specs=[pl.BlockSpec((None, tq, 256), lambda i, j: (i, j, 0)),
                  pl.BlockSpec((None, tq, 256), lambda i, j: (i, j, 1)),
                  pl.BlockSpec((None, tq, LANES), lambda i, j: (i, j, 5)),
                  pl.BlockSpec((None, seq, LANES), lambda i, j: (i, 0, 4)),
                  pl.BlockSpec((None, seq, LANES), lambda i, j: (i, 0, 5)),
                  _resident(ckv_gain.shape), _resident(wuk_t.shape), _resident(wuv.shape),
                  _resident(bias.shape)],
        out_specs=pl.BlockSpec((None, tq, GROUP_WIDTH), lambda i, j: (i, j, 0)),
        out_shape=jax.ShapeDtypeStruct((b, seq, GROUP_WIDTH), F32),
        scratch_shapes=[pltpu.VMEM((seq, KV_RANK), BF16), pltpu.VMEM((seq, LANES), BF16),
                        pltpu.VMEM((tq, seq), I32)],
        compiler_params=_params(("parallel", "arbitrary")),
        name="dsa",
    )(za, za, za, za, za, ckv_gain, wuk_t, wuv, bias)


def _moba_bias_tile(bias_ref, h, d):
    lo = jnp.maximum(2 * d - 1, 0)
    top = jnp.concatenate([bias_ref[h, 2 * d], bias_ref[h, lo]], axis=1)
    bot = jnp.concatenate([bias_ref[h, 2 * d + 1], bias_ref[h, 2 * d]], axis=1)
    return jnp.concatenate([top, bot], axis=0)


def _moba_body(q_ref, k_ref, v_ref, bias_ref, o_ref, kh_sc, vh_sc, km_sc, *, topb):
    blk = MOBA_BLOCK
    n = pl.program_id(1)
    seq = k_ref.shape[0]
    nb = seq // blk
    nheads = N_HEADS_GROUP

    @pl.when(n == 0)
    def _():
        for j in range(nb):
            kb = k_ref[j * blk:(j + 1) * blk, :]
            vb = v_ref[j * blk:(j + 1) * blk, :]
            km = jnp.mean(kb, axis=0, keepdims=True)
            for h in range(nheads):
                sl = slice(h * HEAD_DIM, (h + 1) * HEAD_DIM)
                kh_sc[h, j * blk:(j + 1) * blk, :] = kb[:, sl].astype(BF16)
                vh_sc[h, j * blk:(j + 1) * blk, :] = vb[:, sl].astype(BF16)
                km_sc[h, j:j + 1, :] = km[:, sl]

    q = q_ref[...]
    rowid = lax.broadcasted_iota(I32, (nb, blk), 0)
    eye = (lax.broadcasted_iota(I32, (blk, blk), 0) == lax.broadcasted_iota(I32, (blk, blk), 1)).astype(BF16)
    causal = lax.broadcasted_iota(I32, (blk, blk), 0) >= lax.broadcasted_iota(I32, (blk, blk), 1)
    outs = []
    for h in range(nheads):
        qh = q[:, h * HEAD_DIM:(h + 1) * HEAD_DIM]
        km_hi, km_mid, _ = _split3(km_sc[h])
        q_hi, q_mid, _ = _split3(qh)
        gate = _dot_nt(km_hi, q_hi) + _dot_nt(km_mid, q_hi) + _dot_nt(km_hi, q_mid)
        gate = jnp.where(rowid < n, gate, -jnp.inf)
        rank = jnp.zeros((nb, blk), I32)
        for mth in range(nb):
            gm = gate[mth:mth + 1, :]
            beats = (gm > gate) | ((gm == gate) & (mth < rowid))
            rank = rank + jnp.where(beats, 1, 0)
        sel_t = jnp.where((rank < topb) & (rowid < n), 1.0, 0.0).astype(BF16)
        sel = _dot_nt(eye, sel_t).astype(BF16)

        qs = (qh * ATTN_SCALE).astype(BF16)
        k0 = pl.multiple_of(n * blk, blk)
        s = _dot_nt(qs, kh_sc[h, pl.ds(k0, blk), :]) + _moba_bias_tile(bias_ref, h, 0)
        s = jnp.where(causal, s, NEG)
        m = jnp.max(s, axis=1, keepdims=True)
        p = jnp.exp(s - m)
        l = jnp.sum(p, axis=1, keepdims=True)
        acc = _dot(p.astype(BF16), vh_sc[h, pl.ds(k0, blk), :])

        def past_block(j, carry, h=h, qs=qs, sel=sel):
            m, l, acc = carry
            j0 = pl.multiple_of(j * blk, blk)
            s = _dot_nt(qs, kh_sc[h, pl.ds(j0, blk), :]) + _moba_bias_tile(bias_ref, h, n - j)
            onehot = (lax.broadcasted_iota(I32, (nb, blk), 0) == j).astype(BF16)
            picked = _dot(sel, onehot)
            s = jnp.where(picked > 0.5, s, NEG)
            return _online_softmax_step(s, vh_sc[h, pl.ds(j0, blk), :], m, l, acc)

        m, l, acc = lax.fori_loop(0, n, past_block, (m, l, acc))
        outs.append(acc / l)
    o_ref[...] = jnp.concatenate(outs, axis=1)


def _moba(zb, bias):
    b, seq, _ = zb.shape
    blk = MOBA_BLOCK
    nb = seq // blk
    return pl.pallas_call(
        functools.partial(_moba_body, topb=min(MOBA_TOPK, nb)),
        grid=(b, nb),
        in_specs=[pl.BlockSpec((None, blk, GROUP_WIDTH), lambda i, j: (i, j, 0)),
                  pl.BlockSpec((None, seq, GROUP_WIDTH), lambda i, j: (i, 0, 1)),
                  pl.BlockSpec((None, seq, GROUP_WIDTH), lambda i, j: (i, 0, 2)),
                  _resident(bias.shape)],
        out_specs=pl.BlockSpec((None, blk, GROUP_WIDTH), lambda i, j: (i, j, 0)),
        out_shape=jax.ShapeDtypeStruct((b, seq, GROUP_WIDTH), F32),
        scratch_shapes=[pltpu.VMEM((N_HEADS_GROUP, seq, HEAD_DIM), BF16),
                        pltpu.VMEM((N_HEADS_GROUP, seq, HEAD_DIM), BF16),
                        pltpu.VMEM((N_HEADS_GROUP, nb, HEAD_DIM), F32)],
        compiler_params=_params(("parallel", "arbitrary")),
        name="moba",
    )(zb, zb, zb, bias)


HGRN_STEP = 256
_HGRN_LEVELS = (32, 16, 8, 4, 2, 1)


def _hgrn_constants():
    c = HGRN_CHUNK
    t = np.arange(c)[:, None]
    u = np.arange(c)[None, :]
    sel = [(u <= t),
           (u > t),
           np.ones((8, c), bool)]
    masks = []
    for m in _HGRN_LEVELS:
        r = (t // (2 * m)) * 2 * m + m - 1
        right = t > r
        sel.append(right & (u > r) & (u <= t))
        sel.append((~right) & (u > t) & (u <= r))
        s = u
        same_blk = (t // (2 * m)) == (s // (2 * m))
        masks.append(same_blk & right & (s <= r))
    masks.append(t == u)
    sel = np.concatenate(sel, axis=0).astype(np.float32)
    masks = np.stack([np.tile(mk, (1, N_HEADS_GROUP)) for mk in masks]).astype(np.float32)
    hid = np.arange(GROUP_WIDTH) // HEAD_DIM
    blockdiag = (hid[:, None] == hid[None, :]).astype(np.float32)
    return sel, masks, blockdiag


def _hgrn_body(z_ref, lbl_ref, gain_ref, sel_ref, mask_ref, bd_ref, o_ref, st_sc, *, layer):
    c = HGRN_CHUNK
    w = GROUP_WIDTH
    nheads = N_HEADS_GROUP

    @pl.when(pl.program_id(1) == 0)
    def _():
        st_sc[...] = jnp.zeros(st_sc.shape, F32)

    logits = lbl_ref[...]
    e = jnp.exp(logits - jnp.max(logits, axis=0, keepdims=True))
    lb_w = e / jnp.sum(e, axis=0, keepdims=True)
    lb = jnp.sum(lb_w[0:layer + 1], axis=0, keepdims=True) - lb_w[0:1]

    bd = bd_ref[...] > 0.5
    bd_bf16 = bd_ref[...].astype(BF16)
    sel = sel_ref[...].astype(BF16)

    def blockdiag(x_bf16):
        return jnp.where(bd, jnp.concatenate([x_bf16] * nheads, axis=0), jnp.zeros((), BF16))

    for ci in range(z_ref.shape[0] // c):
        r0 = ci * c
        q = z_ref[r0:r0 + c, 0:w]
        fpre = z_ref[r0:r0 + c, w:2 * w]
        v = z_ref[r0:r0 + c, 2 * w:3 * w]
        g = z_ref[r0:r0 + c, 3 * w:4 * w]
        f = lb + (1.0 - lb) * jax.nn.sigmoid(fpre)
        logf = jnp.log(f)
        kin = 1.0 - f
        hi, mid, lo = _split3(logf)
        decay = jnp.exp(_dot(sel, hi) + _dot(sel, mid) + _dot(sel, lo))
        w_b = decay[0:c]
        w_rest = decay[c:2 * c]
        w_last = decay[2 * c:2 * c + 1]

        a = jnp.zeros((c, w), F32)
        for li in range(len(_HGRN_LEVELS)):
            base = 2 * c + 8 + 2 * c * li
            qm = (q * decay[base:base + c]).astype(BF16)
            km = (kin * decay[base + c:base + 2 * c]).astype(BF16)
            a = a + mask_ref[li] * _dot_nt(qm, blockdiag(km))
        a = a + mask_ref[len(_HGRN_LEVELS)] * _dot_nt(q.astype(BF16), blockdiag(kin.astype(BF16)))

        v_bf16 = v.astype(BF16)
        st = st_sc[...]
        o = _dot(a.astype(BF16), blockdiag(v_bf16)) + _dot_nt((q * w_b).astype(BF16), st.astype(BF16))
        khat = (kin * w_rest).astype(BF16)
        st_sc[...] = st * w_last + jnp.where(bd, _dot_tn(v_bf16, khat), 0.0)

        oo = o * o
        oo_hi = oo.astype(BF16)
        oo_lo = (oo - oo_hi.astype(F32)).astype(BF16)
        ms = (_dot(oo_hi, bd_bf16) + _dot(oo_lo, bd_bf16)) * (1.0 / HEAD_DIM)
        o = o * lax.rsqrt(ms + EPS)
        o_ref[r0:r0 + c, :] = o * gain_ref[...] * (g * jax.nn.sigmoid(g))


def _hgrn(zc, lb_logits, gain, layer):
    b, seq, _ = zc.shape
    sel, masks, blockdiag = _hgrn_constants()
    return pl.pallas_call(
        functools.partial(_hgrn_body, layer=layer),
        grid=(b, seq // HGRN_STEP),
        in_specs=[pl.BlockSpec((None, HGRN_STEP, GC_W), lambda i, j: (i, j, 0)),
                  _resident(lb_logits.shape), _resident(gain.shape),
                  _resident(sel.shape), _resident(masks.shape), _resident(blockdiag.shape)],
        out_specs=pl.BlockSpec((None, HGRN_STEP, GROUP_WIDTH), lambda i, j: (i, j, 0)),
        out_shape=jax.ShapeDtypeStruct((b, seq, GROUP_WIDTH), F32),
        scratch_shapes=[pltpu.VMEM((GROUP_WIDTH, GROUP_WIDTH), F32)],
        compiler_params=_params(("parallel", "arbitrary")),
        name="hgrn",
    )(zc, lb_logits, gain, jnp.asarray(sel), jnp.asarray(masks), jnp.asarray(blockdiag))


def _dilated_body(q_ref, kp_ref, kc_ref, vp_ref, vc_ref, bias_ref, num_ref, m_ref, s_ref):
    span = DIL_SPAN
    n = pl.program_id(2)
    a_i = lax.broadcasted_iota(I32, (span, 2 * span), 0)
    c_i = lax.broadcasted_iota(I32, (span, 2 * span), 1)
    delta = a_i + span - c_i
    mask = (delta >= 0) & (delta <= span) & ((c_i >= span) | (n > 0))
    q = q_ref[...]
    kcat = jnp.concatenate([kp_ref[...], kc_ref[...]], axis=0)
    vcat = jnp.concatenate([vp_ref[...], vc_ref[...]], axis=0)
    nums, ms, ss = [], [], []
    for h in range(N_HEADS_GROUP):
        sl = slice(h * HEAD_DIM, (h + 1) * HEAD_DIM)
        logits = _dot_nt((q[:, sl] * ATTN_SCALE).astype(BF16), kcat[:, sl].astype(BF16)) + bias_ref[h]
        logits = jnp.where(mask, logits, NEG)
        m = jnp.max(logits, axis=1, keepdims=True)
        p = jnp.exp(logits - m)
        s = jnp.sum(p, axis=1, keepdims=True)
        nums.append(_dot(p.astype(BF16), vcat[:, sl].astype(BF16)))
        ms.append(jnp.broadcast_to(m, (span, HEAD_DIM)))
        ss.append(jnp.broadcast_to(s, (span, HEAD_DIM)))
    num_ref[...] = jnp.concatenate(nums, axis=1)
    m_ref[...] = jnp.concatenate(ms, axis=1)
    s_ref[...] = jnp.concatenate(ss, axis=1)


def _dilated_branch(zd, bias_r, r):
    b, seq, _ = zd.shape
    span = DIL_SPAN
    length = seq // r
    nblk = length // span
    zv = zd.reshape(b, length, r * GD_W)
    blocks = GD_W // GROUP_WIDTH

    def spec(which, prev):
        def index(i, j, k):
            row = jnp.maximum(k - 1, 0) if prev else k
            return (i, row, blocks * j + which)
        return pl.BlockSpec((None, span, GROUP_WIDTH), index)

    out_spec = pl.BlockSpec((None, span, GROUP_WIDTH), lambda i, j, k: (i, k, j))
    out_sds = jax.ShapeDtypeStruct((b, length, r * GROUP_WIDTH), F32)
    outs = pl.pallas_call(
        _dilated_body,
        grid=(b, r, nblk),
        in_specs=[spec(0, False), spec(1, True), spec(1, False), spec(2, True), spec(2, False),
                  _resident(bias_r.shape)],
        out_specs=[out_spec] * 3,
        out_shape=[out_sds] * 3,
        compiler_params=_params(("parallel", "parallel", "arbitrary")),
        name=f"dilated_r{r}",
    )(zv, zv, zv, zv, zv, bias_r)
    return [o.reshape(b, seq, GROUP_WIDTH) for o in outs]


def _out_ffn_body(*refs, final):
    x_ref, oa_ref, ob_ref, oc_ref = refs[0:4]
    dil = refs[4:13]
    wo_ref, g_ref, wg_ref, wu_ref, wd_ref = refs[13:18]
    gf_ref = refs[18] if final else None
    o_ref = refs[-1]
    nums, ms, ss = dil[0::3], dil[1::3], dil[2::3]
    m_max = jnp.maximum(jnp.maximum(ms[0][...], ms[1][...]), ms[2][...])
    den = jnp.zeros(m_max.shape, F32)
    num = jnp.zeros(m_max.shape, F32)
    for i in range(3):
        wgt = jnp.exp(ms[i][...] - m_max)
        den = den + wgt * ss[i][...]
        num = num + wgt * nums[i][...]
    od = num / den
    cat = jnp.concatenate([oa_ref[...], ob_ref[...], oc_ref[...], od], axis=1).astype(BF16)
    x = x_ref[...] + _dot(cat, wo_ref[...])
    x = _swiglu_residual(x, g_ref[...], wg_ref, wu_ref, wd_ref)
    if final:
        x = _rms(x, gf_ref[...])
    o_ref[...] = x


def _out_ffn(x2d, mix, dil, wo, gain, wg, wu, wd, gain_final, tm):
    m, d = x2d.shape
    tile = lambda w: pl.BlockSpec((tm, w), lambda i: (i, 0))
    final = gain_final is not None
    args = [x2d, *mix, *dil, wo, gain, wg, wu, wd] + ([gain_final] if final else [])
    in_specs = ([tile(d)] + [tile(GROUP_WIDTH)] * 12
                + [_resident(wo.shape), _resident((1, d)), _resident(wg.shape), _resident(wu.shape),
                   _resident(wd.shape)] + ([_resident((1, d))] if final else []))
    return pl.pallas_call(
        functools.partial(_out_ffn_body, final=final),
        grid=(m // tm,),
        in_specs=in_specs,
        out_specs=tile(d),
        out_shape=jax.ShapeDtypeStruct((m, d), F32),
        compiler_params=_params(("parallel",)),
        name="out_ffn",
    )(*args)


def kernel(x, norm_ffn1, ffn1_gate, ffn1_up, ffn1_down, norm_mix, w_in, ckv_norm, w_kv_up, hgrn_lb_logits,
           hgrn_norm, w_out, norm_ffn2, ffn2_gate, ffn2_up, ffn2_down, rel_bias, norm_final):
    b, seq, d = x.shape
    depth = w_in.shape[0]
    nh = N_HEADS_GROUP
    m = b * seq
    tm = min(512, m)
    topk = min(TOPK_MAX, seq // 4)

    toe_ids = _toeplitz_bucket_ids(seq)
    bias_a = _bias_tiles(rel_bias, toe_ids, 0)
    bias_b = _bias_tiles(rel_bias, toe_ids, nh)
    bias_d = _bias_tiles(rel_bias, _dilated_bucket_ids(), 2 * nh)

    x2d = x.reshape(m, d)
    for l in range(depth):
        x2d = _ffn(x2d, norm_ffn1[l][None], ffn1_gate[l].astype(BF16), ffn1_up[l].astype(BF16),
                   ffn1_down[l].astype(BF16), tm)
        za, zb, zc, zd = _inproj(x2d, norm_mix[l][None], _pack_w_in(w_in[l]), tm)
        za, zb, zc, zd = (z.reshape(b, seq, -1) for z in (za, zb, zc, zd))

        w_up = w_kv_up[l].reshape(KV_RANK, 2, nh, HEAD_DIM)
        wuk_t = jnp.transpose(w_up[:, 0], (1, 2, 0)).astype(BF16)
        wuv = jnp.transpose(w_up[:, 1], (1, 0, 2)).astype(BF16)
        o_a = _dsa(za, ckv_norm[l][None], wuk_t, wuv, bias_a, topk)
        o_b = _moba(zb, bias_b)
        o_c = _hgrn(zc, hgrn_lb_logits, hgrn_norm[l][None], l)
        dil = []
        for bi, (_, r) in enumerate(DILATED_BRANCHES):
            dil += _dilated_branch(zd, bias_d[:, bi], r)

        mix = [o.reshape(m, GROUP_WIDTH) for o in (o_a, o_b, o_c)]
        dil = [o.reshape(m, GROUP_WIDTH) for o in dil]
        gain_final = norm_final[None] if l == depth - 1 else None
        x2d = _out_ffn(x2d, mix, dil, w_out[l].astype(BF16), norm_ffn2[l][None], ffn2_gate[l].astype(BF16),
                       ffn2_up[l].astype(BF16), ffn2_down[l].astype(BF16), gain_final, tm)
    return x2d.reshape(b, seq, d)
```

```python
import functools
import math

import jax
import jax.numpy as jnp
import numpy as np
from jax import lax
from jax.experimental import pallas as pl
from jax.experimental.pallas import tpu as pltpu

F32 = jnp.float32
BF16 = jnp.bfloat16
I32 = jnp.int32

HEAD_DIM = 64
N_GROUPS = 4
GROUP_WIDTH = 256
N_HEADS_GROUP = GROUP_WIDTH // HEAD_DIM
ATTN_SCALE = HEAD_DIM ** -0.5
KV_RANK = 128
IDX_HEADS = 8
IDX_DIM = 32
TOPK_MAX = 256
MOBA_BLOCK = 256
MOBA_TOPK = 3
HGRN_CHUNK = 64
DILATED_BRANCHES = ((128, 1), (512, 4), (2048, 16))
DIL_SPAN = 128
N_BUCKETS = 32
MAX_DISTANCE = 2048
EPS = 1e-6

LANES = 128
VMEM_LIMIT_BYTES = 56 * 1024 * 1024

NEG = -1e30
INT_MIN = -2 ** 31

GA_W, GB_W, GC_W, GD_W = 768, 768, 1024, 768
A_IW_LANE = 32


def _dot(a, b):
    return jnp.dot(a, b, preferred_element_type=F32)


def _dot_nt(a, b):
    return lax.dot_general(a, b, (((1,), (1,)), ((), ())), preferred_element_type=F32)


def _dot_tn(a, b):
    return lax.dot_general(a, b, (((0,), (0,)), ((), ())), preferred_element_type=F32)


def _rms(x, gain):
    return x * lax.rsqrt(jnp.mean(x * x, axis=-1, keepdims=True) + EPS) * gain


def _split3(x):
    hi = x.astype(BF16)
    r1 = x - hi.astype(F32)
    mid = r1.astype(BF16)
    lo = (r1 - mid.astype(F32)).astype(BF16)
    return hi, mid, lo


def _params(sem):
    return pltpu.CompilerParams(dimension_semantics=sem, vmem_limit_bytes=VMEM_LIMIT_BYTES)


def _resident(shape):
    nd = len(shape)
    return pl.BlockSpec(shape, lambda *_: (0,) * nd, pipeline_mode=pl.Buffered(1))


def _bucket_np(dist):
    max_exact = N_BUCKETS // 2
    n = np.maximum(dist, 0)
    nf = np.maximum(n, 1).astype(np.float64)
    large = max_exact + (np.log(nf / max_exact) / math.log(MAX_DISTANCE / max_exact)
                         * (N_BUCKETS - max_exact)).astype(np.int64)
    large = np.minimum(large, N_BUCKETS - 1)
    return np.where(n < max_exact, n, large).astype(np.int32)


def _toeplitz_bucket_ids(seq):
    nj = seq // LANES
    s = np.arange(LANES)[:, None]
    q = np.arange(LANES)[None, :]
    j = np.arange(nj)[:, None, None]
    return _bucket_np(LANES * j + q - s)


def _dilated_bucket_ids():
    a = np.arange(DIL_SPAN)[:, None]
    c = np.arange(2 * DIL_SPAN)[None, :]
    delta = a + DIL_SPAN - c
    return np.stack([_bucket_np(delta * r) for _, r in DILATED_BRANCHES])


def _bias_tiles_body(tab_ref, ids_ref, o_ref, *, head0):
    ids = ids_ref[0]
    c = ids.shape[1]
    for h in range(N_HEADS_GROUP):
        acc = jnp.zeros(ids.shape, F32)
        for k in range(N_BUCKETS):
            acc = jnp.where(ids == k, tab_ref[k, head0 + h], acc)
        o_ref[0, :, h * c:(h + 1) * c] = acc


def _bias_tiles(rel_bias, ids, head0):
    n, r, c = ids.shape
    return pl.pallas_call(
        functools.partial(_bias_tiles_body, head0=head0),
        grid=(n,),
        in_specs=[pl.BlockSpec(memory_space=pltpu.SMEM),
                  pl.BlockSpec((1, r, c), lambda i: (i, 0, 0))],
        out_specs=pl.BlockSpec((1, r, N_HEADS_GROUP * c), lambda i: (i, 0, 0)),
        out_shape=jax.ShapeDtypeStruct((n, r, N_HEADS_GROUP * c), F32),
        compiler_params=_params(("arbitrary",)),
        name="bias_tiles",
    )(rel_bias, jnp.asarray(ids))


FFN_CHUNK = 256


def _swiglu_residual(x, gain, wg_ref, wu_ref, wd_ref):
    h = _rms(x, gain).astype(BF16)
    d_ff = wg_ref.shape[1]
    acc = jnp.zeros(x.shape, F32)
    for c0 in range(0, d_ff, FFN_CHUNK):
        g = _dot(h, wg_ref[:, c0:c0 + FFN_CHUNK])
        u = _dot(h, wu_ref[:, c0:c0 + FFN_CHUNK])
        a = (g * jax.nn.sigmoid(g) * u).astype(BF16)
        acc = acc + _dot(a, wd_ref[c0:c0 + FFN_CHUNK, :])
    return x + 0.5 * acc


def _ffn_body(x_ref, g_ref, wg_ref, wu_ref, wd_ref, o_ref):
    o_ref[...] = _swiglu_residual(x_ref[...], g_ref[...], wg_ref, wu_ref, wd_ref)


def _ffn(x2d, gain, wg, wu, wd, tm):
    m, d = x2d.shape
    return pl.pallas_call(
        _ffn_body,
        grid=(m // tm,),
        in_specs=[pl.BlockSpec((tm, d), lambda i: (i, 0)),
                  _resident((1, d)), _resident(wg.shape), _resident(wu.shape), _resident(wd.shape)],
        out_specs=pl.BlockSpec((tm, d), lambda i: (i, 0)),
        out_shape=jax.ShapeDtypeStruct((m, d), F32),
        compiler_params=_params(("parallel",)),
        name="ffn",
    )(x2d, gain, wg, wu, wd)


def _inproj_body(x_ref, g_ref, w_ref, za_ref, zb_ref, zc_ref, zd_ref):
    h = _rms(x_ref[...], g_ref[...]).astype(BF16)
    c0 = 0
    for ref in (za_ref, zb_ref, zc_ref, zd_ref):
        w = ref.shape[1]
        ref[...] = _dot(h, w_ref[:, c0:c0 + w])
        c0 += w


def _inproj(x2d, gain, w_packed, tm):
    m, d = x2d.shape
    widths = (GA_W, GB_W, GC_W, GD_W)
    return pl.pallas_call(
        _inproj_body,
        grid=(m // tm,),
        in_specs=[pl.BlockSpec((tm, d), lambda i: (i, 0)), _resident((1, d)), _resident(w_packed.shape)],
        out_specs=[pl.BlockSpec((tm, w), lambda i: (i, 0)) for w in widths],
        out_shape=[jax.ShapeDtypeStruct((m, w), F32) for w in widths],
        compiler_params=_params(("parallel",)),
        name="inproj",
    )(x2d, gain, w_packed)


def _pack_w_in(w):
    d = w.shape[0]
    pad = jnp.zeros((d, GA_W - 680), w.dtype)
    return jnp.concatenate([w[:, 0:256], w[:, 384:640], w[:, 256:384], w[:, 640:680], pad, w[:, 680:]],
                           axis=1).astype(BF16)


def _online_softmax_step(s, vt_bf16, m, l, acc):
    m_new = jnp.maximum(m, jnp.max(s, axis=0, keepdims=True))
    alpha = jnp.exp(m - m_new)
    p = jnp.exp(s - m_new)
    l = alpha * l + jnp.sum(p, axis=0, keepdims=True)
    acc = alpha * acc + _dot(vt_bf16, p.astype(BF16))
    return m_new, l, acc


DSA_TQ = 128
DSA_SCORE_GROUP = 2
DSA_COUNT_GROUP = 4
DSA_ATTN_GROUP = 4


def _dsa_body(q_ref, iq_ref, iwq_ref, ckv_ref, ikw_ref, gain_ref, wuk_ref, wuvt_ref, bias_ref,
              o_ref, c_sc, ct_sc, ik_sc, key_sc, *, topk):
    tq = DSA_TQ
    n = pl.program_id(1)
    seq = ckv_ref.shape[0]
    nheads = N_HEADS_GROUP
    lane = lax.broadcasted_iota(I32, (1, LANES), 1)

    @pl.when(n == 0)
    def _():
        for r0 in range(0, seq, tq):
            c = _rms(ckv_ref[r0:r0 + tq, :], gain_ref[...])
            c_sc[r0:r0 + tq, :] = c.astype(BF16)
            ct_sc[:, r0:r0 + tq] = c.T.astype(BF16)
            ik = jnp.where(lane < IDX_DIM, ikw_ref[r0:r0 + tq, :], 0.0)
            ik4 = ik
            for rep in range(1, LANES // IDX_DIM):
                ik4 = ik4 + pltpu.roll(ik, rep * IDX_DIM, 1)
            ik_sc[r0:r0 + tq, :] = ik4.astype(BF16)

    iq = iq_ref[...]
    iw_t = (iwq_ref[...] * (IDX_HEADS * IDX_DIM) ** -0.5).T
    heads_per_group = LANES // IDX_DIM
    lhs = []
    for h in range(IDX_HEADS):
        blk = iq[:, LANES * (h // heads_per_group):LANES * (h // heads_per_group + 1)]
        lo = IDX_DIM * (h % heads_per_group)
        lhs.append(jnp.where((lane >= lo) & (lane < lo + IDX_DIM), blk, 0.0).astype(BF16))
    lhs = jnp.concatenate(lhs, axis=0)
    qpos = n * tq + lax.broadcasted_iota(I32, (1, tq), 1)
    srow = lax.broadcasted_iota(I32, (tq, 1), 0)
    ntiles = n + 1

    def over_key_tiles(group, body, carry):
        ngroups = lax.shift_right_logical(ntiles, group.bit_length() - 1)
        carry = lax.fori_loop(0, ngroups, lambda i, c: body(i * group, group, c), carry)
        return lax.fori_loop(ngroups * group, ntiles, lambda i, c: body(i, 1, c), carry)

    def score_tiles(kt, g, carry):
        k0 = pl.multiple_of(kt * tq, tq)
        s = _dot_nt(ik_sc[pl.ds(k0, g * tq), :], lhs)
        idx = jnp.zeros((g * tq, tq), F32)
        for h in range(IDX_HEADS):
            idx = idx + iw_t[A_IW_LANE + h:A_IW_LANE + h + 1, :] * jnp.maximum(s[:, h * tq:(h + 1) * tq], 0.0)
        bits = pltpu.bitcast(idx, I32)
        keys = bits ^ ((bits >> 31) & 0x7FFFFFFF)
        spos = k0 + lax.broadcasted_iota(I32, (g * tq, 1), 0)
        key_sc[pl.ds(k0, g * tq), :] = jnp.where(spos <= qpos, keys, INT_MIN)
        return carry

    over_key_tiles(DSA_SCORE_GROUP, score_tiles, 0)

    def count(pred):
        def body(kt, g, acc):
            for i in range(g):
                k0 = pl.multiple_of((kt + i) * tq, tq)
                acc = acc + jnp.where(pred(key_sc[pl.ds(k0, tq), :], k0), 1, 0)
            return acc
        acc = over_key_tiles(DSA_COUNT_GROUP, body, jnp.zeros((tq, tq), I32))
        return jnp.sum(acc, axis=0, keepdims=True)

    def bit_iter(i, t):
        cand = t | lax.shift_left(jnp.int32(1), 31 - i)
        cand_s = cand ^ INT_MIN
        cnt = count(lambda keys, k0: keys >= cand_s)
        return jnp.where(cnt >= topk, cand, t)

    t = lax.fori_loop(0, 32, bit_iter, jnp.zeros((1, tq), I32))
    tau = t ^ INT_MIN
    cnt_gt = count(lambda keys, k0: keys > tau)
    cnt_eq = count(lambda keys, k0: keys == tau)
    need = topk - cnt_gt
    tie_q = (cnt_eq > need) & (t != 0)

    @pl.when(jnp.max(tie_q.astype(I32)) > 0)
    def _():
        def pos_iter(i, p):
            cand = p | lax.shift_left(jnp.int32(1), (seq.bit_length() - 1) - i)
            cnt = count(lambda keys, k0: (keys == tau) & (k0 + srow < cand))
            return jnp.where(cnt < need, cand, p)
        p = lax.fori_loop(0, seq.bit_length(), pos_iter, jnp.zeros((1, tq), I32))
        p = jnp.where(tie_q, p, seq)

        def drop(kt, carry):
            k0 = pl.multiple_of(kt * tq, tq)
            keys = key_sc[pl.ds(k0, tq), :]
            key_sc[pl.ds(k0, tq), :] = jnp.where((keys == tau) & (k0 + srow > p), INT_MIN, keys)
            return carry
        lax.fori_loop(0, ntiles, drop, 0)

    tau_sel = jnp.maximum(tau, INT_MIN + 1)

    q = q_ref[...]
    qs = []
    for h in range(nheads):
        qh = q[:, h * HEAD_DIM:(h + 1) * HEAD_DIM].astype(BF16)
        qs.append((_dot(qh, wuk_ref[h]) * ATTN_SCALE).astype(BF16))
    qs = jnp.concatenate(qs, axis=0)
    cols = nheads * tq

    def attn_tiles(kt, g, carry):
        m, l, acc = carry
        k0 = pl.multiple_of(kt * tq, tq)
        s = _dot_nt(c_sc[pl.ds(k0, g * tq), :], qs)
        drop = jnp.where(key_sc[pl.ds(k0, g * tq), :] >= tau_sel, 0.0, NEG)
        bias = jnp.concatenate([bias_ref[n - kt - i] for i in range(g)], axis=0)
        s = s + bias + jnp.concatenate([drop] * nheads, axis=1)
        return _online_softmax_step(s, ct_sc[:, pl.ds(k0, g * tq)], m, l, acc)

    m0 = jnp.full((1, cols), NEG, F32)
    l0 = jnp.zeros((1, cols), F32)
    acc0 = jnp.zeros((KV_RANK, cols), F32)
    m, l, acc = over_key_tiles(DSA_ATTN_GROUP, attn_tiles, (m0, l0, acc0))
    ctx = (acc / l).astype(BF16)
    for h in range(nheads):
        o_ref[h * HEAD_DIM:(h + 1) * HEAD_DIM, :] = _dot(wuvt_ref[h], ctx[:, h * tq:(h + 1) * tq])


def _dsa(za, ckv_gain, wuk_t, wuv_t, bias, topk):
    b, seq, _ = za.shape
    tq = DSA_TQ
    return pl.pallas_call(
        functools.partial(_dsa_body, topk=topk),
        grid=(b, seq // tq),
        in_specs=[pl.BlockSpec((None, tq, 256), lambda i, j: (i, j, 0)),
                  pl.BlockSpec((None, tq, 256), lambda i, j: (i, j, 1)),
                  pl.BlockSpec((None, tq, LANES), lambda i, j: (i, j, 5)),
                  pl.BlockSpec((None, seq, LANES), lambda i, j: (i, 0, 4)),
                  pl.BlockSpec((None, seq, LANES), lambda i, j: (i, 0, 5)),
                  _resident(ckv_gain.shape), _resident(wuk_t.shape), _resident(wuv_t.shape),
                  _resident(bias.shape)],
        out_specs=pl.BlockSpec((None, GROUP_WIDTH, tq), lambda i, j: (i, 0, j)),
        out_shape=jax.ShapeDtypeStruct((b, GROUP_WIDTH, seq), F32),
        scratch_shapes=[pltpu.VMEM((seq, KV_RANK), BF16), pltpu.VMEM((KV_RANK, seq), BF16),
                        pltpu.VMEM((seq, LANES), BF16), pltpu.VMEM((seq, tq), I32)],
        compiler_params=_params(("parallel", "arbitrary")),
        name="dsa",
    )(za, za, za, za, za, ckv_gain, wuk_t, wuv_t, bias)


def _moba_bias_tile(bias_ref, d):
    lo = jnp.maximum(2 * d - 1, 0)
    t_same, t_next, t_prev = bias_ref[2 * d], bias_ref[2 * d + 1], bias_ref[lo]
    top, bot = [], []
    for h in range(N_HEADS_GROUP):
        hs = slice(h * LANES, (h + 1) * LANES)
        top += [t_same[:, hs], t_next[:, hs]]
        bot += [t_prev[:, hs], t_same[:, hs]]
    return jnp.concatenate([jnp.concatenate(top, axis=1), jnp.concatenate(bot, axis=1)], axis=0)


def _moba_body(q_ref, k_ref, v_ref, bias_ref, o_ref, kh_sc, vt_sc, km_sc, sel_sc, *, topb):
    blk = MOBA_BLOCK
    n = pl.program_id(1)
    seq = k_ref.shape[0]
    nb = seq // blk
    nheads = N_HEADS_GROUP

    @pl.when(n == 0)
    def _():
        for j in range(nb):
            kb = k_ref[j * blk:(j + 1) * blk, :]
            vbt = v_ref[j * blk:(j + 1) * blk, :].T
            km = jnp.mean(kb, axis=0, keepdims=True)
            for h in range(nheads):
                sl = slice(h * HEAD_DIM, (h + 1) * HEAD_DIM)
                kh_sc[h, j * blk:(j + 1) * blk, :] = kb[:, sl].astype(BF16)
                vt_sc[h, :, j * blk:(j + 1) * blk] = vbt[sl, :].astype(BF16)
                km_sc[h, j:j + 1, :] = km[:, sl]

    q = q_ref[...]
    rowid = lax.broadcasted_iota(I32, (nb, blk), 0)
    qs = []
    for h in range(nheads):
        qh = q[:, h * HEAD_DIM:(h + 1) * HEAD_DIM]
        km_hi, km_mid, _ = _split3(km_sc[h])
        q_hi, q_mid, _ = _split3(qh)
        gate = _dot_nt(km_hi, q_hi) + _dot_nt(km_mid, q_hi) + _dot_nt(km_hi, q_mid)
        gate = jnp.where(rowid < n, gate, -jnp.inf)
        rank = jnp.zeros((nb, blk), I32)
        for mth in range(nb):
            gm = gate[mth:mth + 1, :]
            beats = (gm > gate) | ((gm == gate) & (mth < rowid))
            rank = rank + jnp.where(beats, 1, 0)
        sel_sc[:, h * blk:(h + 1) * blk] = jnp.where((rank < topb) & (rowid < n), 0.0, NEG)
        qs.append((qh * ATTN_SCALE).astype(BF16))

    def logits(j0, d):
        s = jnp.concatenate([_dot_nt(kh_sc[h, pl.ds(j0, blk), :], qs[h]) for h in range(nheads)], axis=1)
        return s + _moba_bias_tile(bias_ref, d)

    def weighted_values(j0, p):
        p = p.astype(BF16)
        return jnp.concatenate([_dot(vt_sc[h, :, pl.ds(j0, blk)], p[:, h * blk:(h + 1) * blk])
                                for h in range(nheads)], axis=1)

    k0 = pl.multiple_of(n * blk, blk)
    wide = (blk, nheads * blk)
    causal = lax.broadcasted_iota(I32, wide, 0) <= (lax.broadcasted_iota(I32, wide, 1) & (blk - 1))
    s = jnp.where(causal, logits(k0, 0), NEG)
    m = jnp.max(s, axis=0, keepdims=True)
    p = jnp.exp(s - m)
    l = jnp.sum(p, axis=0, keepdims=True)
    acc = weighted_values(k0, p)

    def past_block(j, carry):
        m, l, acc = carry
        j0 = pl.multiple_of(j * blk, blk)
        s = logits(j0, n - j) + sel_sc[pl.ds(j, 1), :]
        m_new = jnp.maximum(m, jnp.max(s, axis=0, keepdims=True))
        alpha = jnp.exp(m - m_new)
        p = jnp.exp(s - m_new)
        l = alpha * l + jnp.sum(p, axis=0, keepdims=True)
        return m_new, l, alpha * acc + weighted_values(j0, p)

    m, l, acc = lax.fori_loop(0, n, past_block, (m, l, acc))
    out = acc / l
    for h in range(nheads):
        o_ref[h * HEAD_DIM:(h + 1) * HEAD_DIM, :] = out[:, h * blk:(h + 1) * blk]


def _moba(zb, bias):
    b, seq, _ = zb.shape
    blk = MOBA_BLOCK
    nb = seq // blk
    return pl.pallas_call(
        functools.partial(_moba_body, topb=min(MOBA_TOPK, nb)),
        grid=(b, nb),
        in_specs=[pl.BlockSpec((None, blk, GROUP_WIDTH), lambda i, j: (i, j, 0)),
                  pl.BlockSpec((None, seq, GROUP_WIDTH), lambda i, j: (i, 0, 1)),
                  pl.BlockSpec((None, seq, GROUP_WIDTH), lambda i, j: (i, 0, 2)),
                  _resident(bias.shape)],
        out_specs=pl.BlockSpec((None, GROUP_WIDTH, blk), lambda i, j: (i, 0, j)),
        out_shape=jax.ShapeDtypeStruct((b, GROUP_WIDTH, seq), F32),
        scratch_shapes=[pltpu.VMEM((N_HEADS_GROUP, seq, HEAD_DIM), BF16),
                        pltpu.VMEM((N_HEADS_GROUP, HEAD_DIM, seq), BF16),
                        pltpu.VMEM((N_HEADS_GROUP, nb, HEAD_DIM), F32),
                        pltpu.VMEM((nb, N_HEADS_GROUP * blk), F32)],
        compiler_params=_params(("parallel", "arbitrary")),
        name="moba",
    )(zb, zb, zb, bias)


HGRN_STEP = 256
_HGRN_LEVELS = (32, 16, 8, 4, 2, 1)


def _hgrn_constants():
    c = HGRN_CHUNK
    t = np.arange(c)[:, None]
    u = np.arange(c)[None, :]
    sel = [(u <= t),
           (u > t),
           np.ones((8, c), bool)]
    masks = []
    for m in _HGRN_LEVELS:
        r = (t // (2 * m)) * 2 * m + m - 1
        right = t > r
        sel.append(right & (u > r) & (u <= t))
        sel.append((~right) & (u > t) & (u <= r))
        s = u
        same_blk = (t // (2 * m)) == (s // (2 * m))
        masks.append(same_blk & right & (s <= r))
    masks.append(t == u)
    sel = np.concatenate(sel, axis=0).astype(np.float32)
    masks = np.stack([np.tile(mk, (1, N_HEADS_GROUP)) for mk in masks]).astype(np.float32)
    hid = np.arange(GROUP_WIDTH) // HEAD_DIM
    blockdiag = (hid[:, None] == hid[None, :]).astype(np.float32)
    return sel, masks, blockdiag


def _hgrn_body(z_ref, lbl_ref, gain_ref, sel_ref, mask_ref, bd_ref, o_ref, st_sc, *, layer):
    c = HGRN_CHUNK
    w = GROUP_WIDTH
    nheads = N_HEADS_GROUP

    @pl.when(pl.program_id(1) == 0)
    def _():
        st_sc[...] = jnp.zeros(st_sc.shape, F32)

    logits = lbl_ref[...]
    e = jnp.exp(logits - jnp.max(logits, axis=0, keepdims=True))
    lb_w = e / jnp.sum(e, axis=0, keepdims=True)
    lb = jnp.sum(lb_w[0:layer + 1], axis=0, keepdims=True) - lb_w[0:1]

    bd = bd_ref[...] > 0.5
    bd_bf16 = bd_ref[...].astype(BF16)
    sel = sel_ref[...].astype(BF16)

    def blockdiag(x_bf16):
        return jnp.where(bd, jnp.concatenate([x_bf16] * nheads, axis=0), jnp.zeros((), BF16))

    for ci in range(z_ref.shape[0] // c):
        r0 = ci * c
        q = z_ref[r0:r0 + c, 0:w]
        fpre = z_ref[r0:r0 + c, w:2 * w]
        v = z_ref[r0:r0 + c, 2 * w:3 * w]
        g = z_ref[r0:r0 + c, 3 * w:4 * w]
        f = lb + (1.0 - lb) * jax.nn.sigmoid(fpre)
        logf = jnp.log(f)
        kin = 1.0 - f
        hi, mid, lo = _split3(logf)
        decay = jnp.exp(_dot(sel, hi) + _dot(sel, mid) + _dot(sel, lo))
        w_b = decay[0:c]
        w_rest = decay[c:2 * c]
        w_last = decay[2 * c:2 * c + 1]

        a = jnp.zeros((c, w), F32)
        for li in range(len(_HGRN_LEVELS)):
            base = 2 * c + 8 + 2 * c * li
            qm = (q * decay[base:base + c]).astype(BF16)
            km = (kin * decay[base + c:base + 2 * c]).astype(BF16)
            a = a + mask_ref[li] * _dot_nt(qm, blockdiag(km))
        a = a + mask_ref[len(_HGRN_LEVELS)] * _dot_nt(q.astype(BF16), blockdiag(kin.astype(BF16)))

        v_bf16 = v.astype(BF16)
        st = st_sc[...]
        o = _dot(a.astype(BF16), blockdiag(v_bf16)) + _dot_nt((q * w_b).astype(BF16), st.astype(BF16))
        khat = (kin * w_rest).astype(BF16)
        st_sc[...] = st * w_last + jnp.where(bd, _dot_tn(v_bf16, khat), 0.0)

        oo = o * o
        oo_hi = oo.astype(BF16)
        oo_lo = (oo - oo_hi.astype(F32)).astype(BF16)
        ms = (_dot(oo_hi, bd_bf16) + _dot(oo_lo, bd_bf16)) * (1.0 / HEAD_DIM)
        o = o * lax.rsqrt(ms + EPS)
        o_ref[r0:r0 + c, :] = o * gain_ref[...] * (g * jax.nn.sigmoid(g))


def _hgrn(zc, lb_logits, gain, layer):
    b, seq, _ = zc.shape
    sel, masks, blockdiag = _hgrn_constants()
    return pl.pallas_call(
        functools.partial(_hgrn_body, layer=layer),
        grid=(b, seq // HGRN_STEP),
        in_specs=[pl.BlockSpec((None, HGRN_STEP, GC_W), lambda i, j: (i, j, 0)),
                  _resident(lb_logits.shape), _resident(gain.shape),
                  _resident(sel.shape), _resident(masks.shape), _resident(blockdiag.shape)],
        out_specs=pl.BlockSpec((None, HGRN_STEP, GROUP_WIDTH), lambda i, j: (i, j, 0)),
        out_shape=jax.ShapeDtypeStruct((b, seq, GROUP_WIDTH), F32),
        scratch_shapes=[pltpu.VMEM((GROUP_WIDTH, GROUP_WIDTH), F32)],
        compiler_params=_params(("parallel", "arbitrary")),
        name="hgrn",
    )(zc, lb_logits, gain, jnp.asarray(sel), jnp.asarray(masks), jnp.asarray(blockdiag))


def _dilated_body(q_ref, kp_ref, kc_ref, vp_ref, vc_ref, bias_ref, num_ref, m_ref, s_ref):
    span = DIL_SPAN
    n = pl.program_id(2)
    a_i = lax.broadcasted_iota(I32, (span, 2 * span), 0)
    c_i = lax.broadcasted_iota(I32, (span, 2 * span), 1)
    delta = a_i + span - c_i
    mask = (delta >= 0) & (delta <= span) & ((c_i >= span) | (n > 0))
    q = q_ref[...]
    kcat = jnp.concatenate([kp_ref[...], kc_ref[...]], axis=0)
    vcat = jnp.concatenate([vp_ref[...], vc_ref[...]], axis=0)
    nums, ms, ss = [], [], []
    for h in range(N_HEADS_GROUP):
        sl = slice(h * HEAD_DIM, (h + 1) * HEAD_DIM)
        bias = bias_ref[:, h * 2 * span:(h + 1) * 2 * span]
        logits = _dot_nt((q[:, sl] * ATTN_SCALE).astype(BF16), kcat[:, sl].astype(BF16)) + bias
        logits = jnp.where(mask, logits, NEG)
        m = jnp.max(logits, axis=1, keepdims=True)
        p = jnp.exp(logits - m)
        s = jnp.sum(p, axis=1, keepdims=True)
        nums.append(_dot(p.astype(BF16), vcat[:, sl].astype(BF16)))
        ms.append(jnp.broadcast_to(m, (span, HEAD_DIM)))
        ss.append(jnp.broadcast_to(s, (span, HEAD_DIM)))
    num_ref[...] = jnp.concatenate(nums, axis=1)
    m_ref[...] = jnp.concatenate(ms, axis=1)
    s_ref[...] = jnp.concatenate(ss, axis=1)


def _dilated_branch(zd, bias_r, r):
    b, seq, _ = zd.shape
    span = DIL_SPAN
    length = seq // r
    nblk = length // span
    zv = zd.reshape(b, length, r * GD_W)
    blocks = GD_W // GROUP_WIDTH

    def spec(which, prev):
        def index(i, j, k):
            row = jnp.maximum(k - 1, 0) if prev else k
            return (i, row, blocks * j + which)
        return pl.BlockSpec((None, span, GROUP_WIDTH), index)

    out_spec = pl.BlockSpec((None, span, GROUP_WIDTH), lambda i, j, k: (i, k, j))
    out_sds = jax.ShapeDtypeStruct((b, length, r * GROUP_WIDTH), F32)
    outs = pl.pallas_call(
        _dilated_body,
        grid=(b, r, nblk),
        in_specs=[spec(0, False), spec(1, True), spec(1, False), spec(2, True), spec(2, False),
                  _resident(bias_r.shape)],
        out_specs=[out_spec] * 3,
        out_shape=[out_sds] * 3,
        compiler_params=_params(("parallel", "parallel", "arbitrary")),
        name=f"dilated_r{r}",
    )(zv, zv, zv, zv, zv, bias_r)
    return [o.reshape(b, seq, GROUP_WIDTH) for o in outs]


def _out_ffn_body(*refs, final):
    x_ref, oat_ref, obt_ref, oc_ref = refs[0:4]
    dil = refs[4:13]
    wo_ref, g_ref, wg_ref, wu_ref, wd_ref = refs[13:18]
    gf_ref = refs[18] if final else None
    o_ref = refs[-1]
    gw = GROUP_WIDTH
    nums, ms, ss = dil[0::3], dil[1::3], dil[2::3]
    m_max = jnp.maximum(jnp.maximum(ms[0][...], ms[1][...]), ms[2][...])
    den = jnp.zeros(m_max.shape, F32)
    num = jnp.zeros(m_max.shape, F32)
    for i in range(3):
        wgt = jnp.exp(ms[i][...] - m_max)
        den = den + wgt * ss[i][...]
        num = num + wgt * nums[i][...]
    od = num / den
    cd = jnp.concatenate([oc_ref[...], od], axis=1).astype(BF16)
    x = (x_ref[...] + _dot_tn(oat_ref[...].astype(BF16), wo_ref[0:gw, :])
         + _dot_tn(obt_ref[...].astype(BF16), wo_ref[gw:2 * gw, :]) + _dot(cd, wo_ref[2 * gw:4 * gw, :]))
    x = _swiglu_residual(x, g_ref[...], wg_ref, wu_ref, wd_ref)
    if final:
        x = _rms(x, gf_ref[...])
    o_ref[...] = x


def _out_ffn(x2d, o_at, o_bt, o_c, dil, wo, gain, wg, wu, wd, gain_final, tm):
    m, d = x2d.shape
    seq = o_at.shape[2]
    per_b = seq // tm
    tile = lambda w: pl.BlockSpec((tm, w), lambda i: (i, 0))
    tile_t = pl.BlockSpec((None, GROUP_WIDTH, tm), lambda i: (i // per_b, 0, i % per_b))
    final = gain_final is not None
    args = [x2d, o_at, o_bt, o_c, *dil, wo, gain, wg, wu, wd] + ([gain_final] if final else [])
    in_specs = ([tile(d), tile_t, tile_t] + [tile(GROUP_WIDTH)] * 10
                + [_resident(wo.shape), _resident((1, d)), _resident(wg.shape), _resident(wu.shape),
                   _resident(wd.shape)] + ([_resident((1, d))] if final else []))
    return pl.pallas_call(
        functools.partial(_out_ffn_body, final=final),
        grid=(m // tm,),
        in_specs=in_specs,
        out_specs=tile(d),
        out_shape=jax.ShapeDtypeStruct((m, d), F32),
        compiler_params=_params(("parallel",)),
        name="out_ffn",
    )(*args)


def kernel(x, norm_ffn1, ffn1_gate, ffn1_up, ffn1_down, norm_mix, w_in, ckv_norm, w_kv_up, hgrn_lb_logits,
           hgrn_norm, w_out, norm_ffn2, ffn2_gate, ffn2_up, ffn2_down, rel_bias, norm_final):
    b, seq, d = x.shape
    depth = w_in.shape[0]
    nh = N_HEADS_GROUP
    m = b * seq
    tm = min(512, seq)
    topk = min(TOPK_MAX, seq // 4)

    toe_ids = _toeplitz_bucket_ids(seq)
    bias_a = _bias_tiles(rel_bias, toe_ids, 0)
    bias_b = _bias_tiles(rel_bias, toe_ids, nh)
    bias_d = _bias_tiles(rel_bias, _dilated_bucket_ids(), 2 * nh)

    x2d = x.reshape(m, d)
    for l in range(depth):
        x2d = _ffn(x2d, norm_ffn1[l][None], ffn1_gate[l].astype(BF16), ffn1_up[l].astype(BF16),
                   ffn1_down[l].astype(BF16), tm)
        za, zb, zc, zd = _inproj(x2d, norm_mix[l][None], _pack_w_in(w_in[l]), tm)
        za, zb, zc, zd = (z.reshape(b, seq, -1) for z in (za, zb, zc, zd))

        w_up = w_kv_up[l].reshape(KV_RANK, 2, nh, HEAD_DIM)
        wuk_t = jnp.transpose(w_up[:, 0], (1, 2, 0)).astype(BF16)
        wuv_t = jnp.transpose(w_up[:, 1], (1, 2, 0)).astype(BF16)
        o_at = _dsa(za, ckv_norm[l][None], wuk_t, wuv_t, bias_a, topk)
        o_bt = _moba(zb, bias_b)
        o_c = _hgrn(zc, hgrn_lb_logits, hgrn_norm[l][None], l).reshape(m, GROUP_WIDTH)
        dil = []
        for bi, (_, r) in enumerate(DILATED_BRANCHES):
            dil += [o.reshape(m, GROUP_WIDTH) for o in _dilated_branch(zd, bias_d[bi], r)]

        gain_final = norm_final[None] if l == depth - 1 else None
        x2d = _out_ffn(x2d, o_at, o_bt, o_c, dil, w_out[l].astype(BF16), norm_ffn2[l][None],
                       ffn2_gate[l].astype(BF16), ffn2_up[l].astype(BF16), ffn2_down[l].astype(BF16),
                       gain_final, tm)
    return x2d.reshape(b, seq, d)
```

```python
import functools
import math

import jax
import jax.numpy as jnp
import numpy as np
from jax import lax
from jax.experimental import pallas as pl
from jax.experimental.pallas import tpu as pltpu

F32 = jnp.float32
BF16 = jnp.bfloat16
I32 = jnp.int32

HEAD_DIM = 64
N_GROUPS = 4
GROUP_WIDTH = 256
N_HEADS_GROUP = GROUP_WIDTH // HEAD_DIM
ATTN_SCALE = HEAD_DIM ** -0.5
KV_RANK = 128
IDX_HEADS = 8
IDX_DIM = 32
TOPK_MAX = 256
MOBA_BLOCK = 256
MOBA_TOPK = 3
HGRN_CHUNK = 64
DILATED_BRANCHES = ((128, 1), (512, 4), (2048, 16))
DIL_SPAN = 128
N_BUCKETS = 32
MAX_DISTANCE = 2048
EPS = 1e-6

LANES = 128
VMEM_LIMIT_BYTES = 56 * 1024 * 1024

NEG = -1e30
INT_MIN = -2 ** 31

GA_W, GB_W, GC_W, GD_W = 768, 768, 1024, 768
A_IW_LANE = 32


def _dot(a, b):
    return jnp.dot(a, b, preferred_element_type=F32)


def _dot_nt(a, b):
    return lax.dot_general(a, b, (((1,), (1,)), ((), ())), preferred_element_type=F32)


def _dot_tn(a, b):
    return lax.dot_general(a, b, (((0,), (0,)), ((), ())), preferred_element_type=F32)


def _rms(x, gain):
    return x * lax.rsqrt(jnp.mean(x * x, axis=-1, keepdims=True) + EPS) * gain


def _split3(x):
    hi = x.astype(BF16)
    r1 = x - hi.astype(F32)
    mid = r1.astype(BF16)
    lo = (r1 - mid.astype(F32)).astype(BF16)
    return hi, mid, lo


def _params(sem):
    return pltpu.CompilerParams(dimension_semantics=sem, vmem_limit_bytes=VMEM_LIMIT_BYTES)


def _resident(shape):
    nd = len(shape)
    return pl.BlockSpec(shape, lambda *_: (0,) * nd, pipeline_mode=pl.Buffered(1))


def _bucket_np(dist):
    max_exact = N_BUCKETS // 2
    n = np.maximum(dist, 0)
    nf = np.maximum(n, 1).astype(np.float64)
    large = max_exact + (np.log(nf / max_exact) / math.log(MAX_DISTANCE / max_exact)
                         * (N_BUCKETS - max_exact)).astype(np.int64)
    large = np.minimum(large, N_BUCKETS - 1)
    return np.where(n < max_exact, n, large).astype(np.int32)


def _toeplitz_bucket_ids(seq):
    nj = seq // LANES
    s = np.arange(LANES)[:, None]
    q = np.arange(LANES)[None, :]
    j = np.arange(nj)[:, None, None]
    return _bucket_np(LANES * j + q - s)


def _dilated_bucket_ids():
    a = np.arange(DIL_SPAN)[:, None]
    c = np.arange(2 * DIL_SPAN)[None, :]
    delta = a + DIL_SPAN - c
    return np.stack([_bucket_np(delta * r) for _, r in DILATED_BRANCHES])


def _bias_tiles_body(tab_ref, ids_ref, o_ref, *, head0):
    ids = ids_ref[0]
    c = ids.shape[1]
    for h in range(N_HEADS_GROUP):
        acc = jnp.zeros(ids.shape, F32)
        for k in range(N_BUCKETS):
            acc = jnp.where(ids == k, tab_ref[k, head0 + h], acc)
        o_ref[0, :, h * c:(h + 1) * c] = acc


def _bias_tiles(rel_bias, ids, head0):
    n, r, c = ids.shape
    return pl.pallas_call(
        functools.partial(_bias_tiles_body, head0=head0),
        grid=(n,),
        in_specs=[pl.BlockSpec(memory_space=pltpu.SMEM),
                  pl.BlockSpec((1, r, c), lambda i: (i, 0, 0))],
        out_specs=pl.BlockSpec((1, r, N_HEADS_GROUP * c), lambda i: (i, 0, 0)),
        out_shape=jax.ShapeDtypeStruct((n, r, N_HEADS_GROUP * c), F32),
        compiler_params=_params(("arbitrary",)),
        name="bias_tiles",
    )(rel_bias, jnp.asarray(ids))


FFN_CHUNK = 256


def _swiglu_residual(x, gain, wg_ref, wu_ref, wd_ref):
    h = _rms(x, gain).astype(BF16)
    d_ff = wg_ref.shape[1]
    acc = jnp.zeros(x.shape, F32)
    for c0 in range(0, d_ff, FFN_CHUNK):
        g = _dot(h, wg_ref[:, c0:c0 + FFN_CHUNK])
        u = _dot(h, wu_ref[:, c0:c0 + FFN_CHUNK])
        a = (g * jax.nn.sigmoid(g) * u).astype(BF16)
        acc = acc + _dot(a, wd_ref[c0:c0 + FFN_CHUNK, :])
    return x + 0.5 * acc


def _ffn_body(x_ref, g_ref, wg_ref, wu_ref, wd_ref, o_ref):
    o_ref[...] = _swiglu_residual(x_ref[...], g_ref[...], wg_ref, wu_ref, wd_ref)


def _ffn(x2d, gain, wg, wu, wd, tm):
    m, d = x2d.shape
    return pl.pallas_call(
        _ffn_body,
        grid=(m // tm,),
        in_specs=[pl.BlockSpec((tm, d), lambda i: (i, 0)),
                  _resident((1, d)), _resident(wg.shape), _resident(wu.shape), _resident(wd.shape)],
        out_specs=pl.BlockSpec((tm, d), lambda i: (i, 0)),
        out_shape=jax.ShapeDtypeStruct((m, d), F32),
        compiler_params=_params(("parallel",)),
        name="ffn",
    )(x2d, gain, wg, wu, wd)


def _inproj_body(x_ref, g_ref, w_ref, za_ref, zb_ref, zc_ref, zd_ref):
    h = _rms(x_ref[...], g_ref[...]).astype(BF16)
    c0 = 0
    for ref in (za_ref, zb_ref, zc_ref, zd_ref):
        w = ref.shape[1]
        ref[...] = _dot(h, w_ref[:, c0:c0 + w])
        c0 += w


def _inproj(x2d, gain, w_packed, tm):
    m, d = x2d.shape
    widths = (GA_W, GB_W, GC_W, GD_W)
    return pl.pallas_call(
        _inproj_body,
        grid=(m // tm,),
        in_specs=[pl.BlockSpec((tm, d), lambda i: (i, 0)), _resident((1, d)), _resident(w_packed.shape)],
        out_specs=[pl.BlockSpec((tm, w), lambda i: (i, 0)) for w in widths],
        out_shape=[jax.ShapeDtypeStruct((m, w), F32) for w in widths],
        compiler_params=_params(("parallel",)),
        name="inproj",
    )(x2d, gain, w_packed)


def _pack_w_in(w):
    d = w.shape[0]
    pad = jnp.zeros((d, GA_W - 680), w.dtype)
    return jnp.concatenate([w[:, 0:256], w[:, 384:640], w[:, 256:384], w[:, 640:680], pad, w[:, 680:]],
                           axis=1).astype(BF16)


def _online_softmax_step(s, vt_bf16, m, l, acc):
    m_new = jnp.maximum(m, jnp.max(s, axis=0, keepdims=True))
    alpha = jnp.exp(m - m_new)
    p = jnp.exp(s - m_new)
    l = alpha * l + jnp.sum(p, axis=0, keepdims=True)
    acc = alpha * acc + _dot(vt_bf16, p.astype(BF16))
    return m_new, l, acc


DSA_TQ = 128
DSA_TILE_PAD = 4
DSA_SCORE_GROUP = 4
DSA_COUNT_GROUP = 4
DSA_ATTN_GROUP = 4


def _dsa_body(q_ref, iq_ref, iwq_ref, ckv_ref, ikw_ref, gain_ref, wuk_ref, wuvt_ref, bias_ref,
              o_ref, c_sc, ct_sc, ik_sc, key_sc, *, topk):
    tq = DSA_TQ
    n = pl.program_id(1)
    seq = ckv_ref.shape[0]
    nheads = N_HEADS_GROUP
    lane = lax.broadcasted_iota(I32, (1, LANES), 1)

    @pl.when(n == 0)
    def _():
        for r0 in range(0, seq, tq):
            c = _rms(ckv_ref[r0:r0 + tq, :], gain_ref[...])
            c_sc[r0:r0 + tq, :] = c.astype(BF16)
            ct_sc[:, r0:r0 + tq] = c.T.astype(BF16)
            ik = jnp.where(lane < IDX_DIM, ikw_ref[r0:r0 + tq, :], 0.0)
            ik4 = ik
            for rep in range(1, LANES // IDX_DIM):
                ik4 = ik4 + pltpu.roll(ik, rep * IDX_DIM, 1)
            ik_sc[r0:r0 + tq, :] = ik4.astype(BF16)

    iq = iq_ref[...]
    iw_t = (iwq_ref[...] * (IDX_HEADS * IDX_DIM) ** -0.5).T
    heads_per_group = LANES // IDX_DIM
    lhs = []
    for h in range(IDX_HEADS):
        blk = iq[:, LANES * (h // heads_per_group):LANES * (h // heads_per_group + 1)]
        lo = IDX_DIM * (h % heads_per_group)
        lhs.append(jnp.where((lane >= lo) & (lane < lo + IDX_DIM), blk, 0.0).astype(BF16))
    lhs = jnp.concatenate(lhs, axis=0)
    qpos = n * tq + lax.broadcasted_iota(I32, (1, tq), 1)
    srow = lax.broadcasted_iota(I32, (tq, 1), 0)
    pad_shift = DSA_TILE_PAD.bit_length() - 1
    ntiles = lax.shift_left(lax.shift_right_logical(n + DSA_TILE_PAD, pad_shift), pad_shift)

    def over_key_tiles(group, body, carry):
        ngroups = lax.shift_right_logical(ntiles, group.bit_length() - 1)
        return lax.fori_loop(0, ngroups, lambda i, c: body(i * group, group, c), carry)

    def score_tiles(kt, g, carry):
        k0 = pl.multiple_of(kt * tq, tq)
        s = _dot_nt(ik_sc[pl.ds(k0, g * tq), :], lhs)
        idx = jnp.zeros((g * tq, tq), F32)
        for h in range(IDX_HEADS):
            idx = idx + iw_t[A_IW_LANE + h:A_IW_LANE + h + 1, :] * jnp.maximum(s[:, h * tq:(h + 1) * tq], 0.0)
        bits = pltpu.bitcast(idx, I32)
        keys = bits ^ ((bits >> 31) & 0x7FFFFFFF)
        spos = k0 + lax.broadcasted_iota(I32, (g * tq, 1), 0)
        key_sc[pl.ds(k0, g * tq), :] = jnp.where(spos <= qpos, keys, INT_MIN)
        return carry

    over_key_tiles(DSA_SCORE_GROUP, score_tiles, 0)

    def count(pred):
        def body(kt, g, acc):
            for i in range(g):
                k0 = pl.multiple_of((kt + i) * tq, tq)
                acc = acc + jnp.where(pred(key_sc[pl.ds(k0, tq), :], k0), 1, 0)
            return acc
        acc = over_key_tiles(DSA_COUNT_GROUP, body, jnp.zeros((tq, tq), I32))
        return jnp.sum(acc, axis=0, keepdims=True)

    def bit_iter(i, t):
        cand = t | lax.shift_left(jnp.int32(1), 31 - i)
        cand_s = cand ^ INT_MIN
        cnt = count(lambda keys, k0: keys >= cand_s)
        return jnp.where(cnt >= topk, cand, t)

    t = lax.fori_loop(0, 32, bit_iter, jnp.zeros((1, tq), I32))
    tau = t ^ INT_MIN
    cnt_gt = count(lambda keys, k0: keys > tau)
    cnt_eq = count(lambda keys, k0: keys == tau)
    need = topk - cnt_gt
    tie_q = (cnt_eq > need) & (t != 0)

    @pl.when(jnp.max(tie_q.astype(I32)) > 0)
    def _():
        def pos_iter(i, p):
            cand = p | lax.shift_left(jnp.int32(1), (seq.bit_length() - 1) - i)
            cnt = count(lambda keys, k0: (keys == tau) & (k0 + srow < cand))
            return jnp.where(cnt < need, cand, p)
        p = lax.fori_loop(0, seq.bit_length(), pos_iter, jnp.zeros((1, tq), I32))
        p = jnp.where(tie_q, p, seq)

        def drop(kt, carry):
            k0 = pl.multiple_of(kt * tq, tq)
            keys = key_sc[pl.ds(k0, tq), :]
            key_sc[pl.ds(k0, tq), :] = jnp.where((keys == tau) & (k0 + srow > p), INT_MIN, keys)
            return carry
        lax.fori_loop(0, ntiles, drop, 0)

    tau_sel = jnp.maximum(tau, INT_MIN + 1)

    q = q_ref[...]
    qs = []
    for h in range(nheads):
        qh = q[:, h * HEAD_DIM:(h + 1) * HEAD_DIM].astype(BF16)
        qs.append((_dot(qh, wuk_ref[h]) * ATTN_SCALE).astype(BF16))
    qs = jnp.concatenate(qs, axis=0)
    cols = nheads * tq

    def attn_tiles(kt, g, carry):
        m, l, acc = carry
        k0 = pl.multiple_of(kt * tq, tq)
        s = _dot_nt(c_sc[pl.ds(k0, g * tq), :], qs)
        drop = jnp.where(key_sc[pl.ds(k0, g * tq), :] >= tau_sel, 0.0, NEG)
        bias = jnp.concatenate([bias_ref[jnp.maximum(n - kt - i, 0)] for i in range(g)], axis=0)
        s = s + bias + jnp.concatenate([drop] * nheads, axis=1)
        return _online_softmax_step(s, ct_sc[:, pl.ds(k0, g * tq)], m, l, acc)

    m0 = jnp.full((1, cols), NEG, F32)
    l0 = jnp.zeros((1, cols), F32)
    acc0 = jnp.zeros((KV_RANK, cols), F32)
    m, l, acc = over_key_tiles(DSA_ATTN_GROUP, attn_tiles, (m0, l0, acc0))
    ctx = (acc / l).astype(BF16)
    for h in range(nheads):
        o_ref[h * HEAD_DIM:(h + 1) * HEAD_DIM, :] = _dot(wuvt_ref[h], ctx[:, h * tq:(h + 1) * tq])


def _dsa(za, ckv_gain, wuk_t, wuv_t, bias, topk):
    b, seq, _ = za.shape
    tq = DSA_TQ
    assert seq % (tq * DSA_TILE_PAD) == 0
    return pl.pallas_call(
        functools.partial(_dsa_body, topk=topk),
        grid=(b, seq // tq),
        in_specs=[pl.BlockSpec((None, tq, 256), lambda i, j: (i, j, 0)),
                  pl.BlockSpec((None, tq, 256), lambda i, j: (i, j, 1)),
                  pl.BlockSpec((None, tq, LANES), lambda i, j: (i, j, 5)),
                  pl.BlockSpec((None, seq, LANES), lambda i, j: (i, 0, 4)),
                  pl.BlockSpec((None, seq, LANES), lambda i, j: (i, 0, 5)),
                  _resident(ckv_gain.shape), _resident(wuk_t.shape), _resident(wuv_t.shape),
                  _resident(bias.shape)],
        out_specs=pl.BlockSpec((None, GROUP_WIDTH, tq), lambda i, j: (i, 0, j)),
        out_shape=jax.ShapeDtypeStruct((b, GROUP_WIDTH, seq), F32),
        scratch_shapes=[pltpu.VMEM((seq, KV_RANK), BF16), pltpu.VMEM((KV_RANK, seq), BF16),
                        pltpu.VMEM((seq, LANES), BF16), pltpu.VMEM((seq, tq), I32)],
        compiler_params=_params(("parallel", "arbitrary")),
        name="dsa",
    )(za, za, za, za, za, ckv_gain, wuk_t, wuv_t, bias)


def _moba_bias_tile(bias_ref, d):
    lo = jnp.maximum(2 * d - 1, 0)
    t_same, t_next, t_prev = bias_ref[2 * d], bias_ref[2 * d + 1], bias_ref[lo]
    top, bot = [], []
    for h in range(N_HEADS_GROUP):
        hs = slice(h * LANES, (h + 1) * LANES)
        top += [t_same[:, hs], t_next[:, hs]]
        bot += [t_prev[:, hs], t_same[:, hs]]
    return jnp.concatenate([jnp.concatenate(top, axis=1), jnp.concatenate(bot, axis=1)], axis=0)


def _moba_body(q_ref, k_ref, v_ref, bias_ref, o_ref, kh_sc, vt_sc, km_sc, sel_sc, *, topb):
    blk = MOBA_BLOCK
    n = pl.program_id(1)
    seq = k_ref.shape[0]
    nb = seq // blk
    nheads = N_HEADS_GROUP

    @pl.when(n == 0)
    def _():
        for j in range(nb):
            kb = k_ref[j * blk:(j + 1) * blk, :]
            vbt = v_ref[j * blk:(j + 1) * blk, :].T
            km = jnp.mean(kb, axis=0, keepdims=True)
            for h in range(nheads):
                sl = slice(h * HEAD_DIM, (h + 1) * HEAD_DIM)
                kh_sc[h, j * blk:(j + 1) * blk, :] = kb[:, sl].astype(BF16)
                vt_sc[h, :, j * blk:(j + 1) * blk] = vbt[sl, :].astype(BF16)
                km_sc[h, j:j + 1, :] = km[:, sl]

    q = q_ref[...]
    rowid = lax.broadcasted_iota(I32, (nb, blk), 0)
    qs = []
    for h in range(nheads):
        qh = q[:, h * HEAD_DIM:(h + 1) * HEAD_DIM]
        km_hi, km_mid, _ = _split3(km_sc[h])
        q_hi, q_mid, _ = _split3(qh)
        gate = _dot_nt(km_hi, q_hi) + _dot_nt(km_mid, q_hi) + _dot_nt(km_hi, q_mid)
        gate = jnp.where(rowid < n, gate, -jnp.inf)
        rank = jnp.zeros((nb, blk), I32)
        for mth in range(nb):
            gm = gate[mth:mth + 1, :]
            beats = (gm > gate) | ((gm == gate) & (mth < rowid))
            rank = rank + jnp.where(beats, 1, 0)
        sel_sc[:, h * blk:(h + 1) * blk] = jnp.where((rank < topb) & (rowid < n), 0.0, NEG)
        qs.append((qh * ATTN_SCALE).astype(BF16))

    def logits(j0, d):
        s = jnp.concatenate([_dot_nt(kh_sc[h, pl.ds(j0, blk), :], qs[h]) for h in range(nheads)], axis=1)
        return s + _moba_bias_tile(bias_ref, d)

    def weighted_values(j0, p):
        p = p.astype(BF16)
        return jnp.concatenate([_dot(vt_sc[h, :, pl.ds(j0, blk)], p[:, h * blk:(h + 1) * blk])
                                for h in range(nheads)], axis=1)

    k0 = pl.multiple_of(n * blk, blk)
    wide = (blk, nheads * blk)
    causal = lax.broadcasted_iota(I32, wide, 0) <= (lax.broadcasted_iota(I32, wide, 1) & (blk - 1))
    s = jnp.where(causal, logits(k0, 0), NEG)
    m = jnp.max(s, axis=0, keepdims=True)
    p = jnp.exp(s - m)
    l = jnp.sum(p, axis=0, keepdims=True)
    acc = weighted_values(k0, p)

    def past_block(j, carry):
        m, l, acc = carry
        j0 = pl.multiple_of(j * blk, blk)
        s = logits(j0, n - j) + sel_sc[pl.ds(j, 1), :]
        m_new = jnp.maximum(m, jnp.max(s, axis=0, keepdims=True))
        alpha = jnp.exp(m - m_new)
        p = jnp.exp(s - m_new)
        l = alpha * l + jnp.sum(p, axis=0, keepdims=True)
        return m_new, l, alpha * acc + weighted_values(j0, p)

    m, l, acc = lax.fori_loop(0, n, past_block, (m, l, acc))
    out = acc / l
    for h in range(nheads):
        o_ref[h * HEAD_DIM:(h + 1) * HEAD_DIM, :] = out[:, h * blk:(h + 1) * blk]


def _moba(zb, bias):
    b, seq, _ = zb.shape
    blk = MOBA_BLOCK
    nb = seq // blk
    return pl.pallas_call(
        functools.partial(_moba_body, topb=min(MOBA_TOPK, nb)),
        grid=(b, nb),
        in_specs=[pl.BlockSpec((None, blk, GROUP_WIDTH), lambda i, j: (i, j, 0)),
                  pl.BlockSpec((None, seq, GROUP_WIDTH), lambda i, j: (i, 0, 1)),
                  pl.BlockSpec((None, seq, GROUP_WIDTH), lambda i, j: (i, 0, 2)),
                  _resident(bias.shape)],
        out_specs=pl.BlockSpec((None, GROUP_WIDTH, blk), lambda i, j: (i, 0, j)),
        out_shape=jax.ShapeDtypeStruct((b, GROUP_WIDTH, seq), F32),
        scratch_shapes=[pltpu.VMEM((N_HEADS_GROUP, seq, HEAD_DIM), BF16),
                        pltpu.VMEM((N_HEADS_GROUP, HEAD_DIM, seq), BF16),
                        pltpu.VMEM((N_HEADS_GROUP, nb, HEAD_DIM), F32),
                        pltpu.VMEM((nb, N_HEADS_GROUP * blk), F32)],
        compiler_params=_params(("parallel", "arbitrary")),
        name="moba",
    )(zb, zb, zb, bias)


HGRN_STEP = 256
_HGRN_LEVELS = (32, 16, 8, 4, 2, 1)


def _hgrn_constants():
    c = HGRN_CHUNK
    t = np.arange(c)[:, None]
    u = np.arange(c)[None, :]
    sel = [(u <= t),
           (u > t),
           np.ones((8, c), bool)]
    masks = []
    for m in _HGRN_LEVELS:
        r = (t // (2 * m)) * 2 * m + m - 1
        right = t > r
        sel.append(right & (u > r) & (u <= t))
        sel.append((~right) & (u > t) & (u <= r))
        s = u
        same_blk = (t // (2 * m)) == (s // (2 * m))
        masks.append(same_blk & right & (s <= r))
    masks.append(t == u)
    sel = np.concatenate(sel, axis=0).astype(np.float32)
    masks = np.stack([np.tile(mk, (HGRN_STEP // c, N_HEADS_GROUP)) for mk in masks]).astype(np.float32)
    hid = np.arange(GROUP_WIDTH) // HEAD_DIM
    blockdiag = (hid[:, None] == hid[None, :]).astype(np.float32)
    return sel, masks, blockdiag


def _hgrn_body(z_ref, lbl_ref, gain_ref, sel_ref, mask_ref, bd_ref, o_ref, st_sc, *, layer):
    c = HGRN_CHUNK
    w = GROUP_WIDTH
    nheads = N_HEADS_GROUP

    @pl.when(pl.program_id(1) == 0)
    def _():
        st_sc[...] = jnp.zeros(st_sc.shape, F32)

    logits = lbl_ref[...]
    e = jnp.exp(logits - jnp.max(logits, axis=0, keepdims=True))
    lb_w = e / jnp.sum(e, axis=0, keepdims=True)
    lb = jnp.sum(lb_w[0:layer + 1], axis=0, keepdims=True) - lb_w[0:1]

    bd = bd_ref[...] > 0.5
    bd_bf16 = bd_ref[...].astype(BF16)
    sel = sel_ref[...].astype(BF16)

    def blockdiag(x_bf16):
        return jnp.where(bd, jnp.concatenate([x_bf16] * nheads, axis=0), jnp.zeros((), BF16))

    chunks = [slice(r0, r0 + c) for r0 in range(0, z_ref.shape[0], c)]

    def per_chunk(fn):
        return jnp.concatenate([fn(rc) for rc in chunks], axis=0)

    q = z_ref[:, 0:w]
    f = lb + (1.0 - lb) * jax.nn.sigmoid(z_ref[:, w:2 * w])
    kin = 1.0 - f
    hi, mid, lo = _split3(jnp.log(f))
    decays = [jnp.exp(_dot(sel, hi[rc]) + _dot(sel, mid[rc]) + _dot(sel, lo[rc])) for rc in chunks]

    def decay_rows(base):
        return jnp.concatenate([d[base:base + c] for d in decays], axis=0)

    a = jnp.zeros(q.shape, F32)
    for li in range(len(_HGRN_LEVELS)):
        base = 2 * c + 8 + 2 * c * li
        qm = (q * decay_rows(base)).astype(BF16)
        km = (kin * decay_rows(base + c)).astype(BF16)
        a = a + mask_ref[li] * per_chunk(lambda rc: _dot_nt(qm[rc], blockdiag(km[rc])))
    q_bf16, kin_bf16 = q.astype(BF16), kin.astype(BF16)
    a = a + mask_ref[len(_HGRN_LEVELS)] * per_chunk(lambda rc: _dot_nt(q_bf16[rc], blockdiag(kin_bf16[rc])))

    a_bf16 = a.astype(BF16)
    v_bf16 = z_ref[:, 2 * w:3 * w].astype(BF16)
    o = per_chunk(lambda rc: _dot(a_bf16[rc], blockdiag(v_bf16[rc])))
    khat = (kin * decay_rows(c)).astype(BF16)
    deltas = [jnp.where(bd, _dot_tn(v_bf16[rc], khat[rc]), 0.0) for rc in chunks]
    q_dec = (q * decay_rows(0)).astype(BF16)
    st = st_sc[...]
    carried = []
    for ci, rc in enumerate(chunks):
        carried.append(_dot_nt(q_dec[rc], st.astype(BF16)))
        st = st * decays[ci][2 * c:2 * c + 1] + deltas[ci]
    st_sc[...] = st
    o = o + jnp.concatenate(carried, axis=0)

    oo = o * o
    oo_hi = oo.astype(BF16)
    oo_lo = (oo - oo_hi.astype(F32)).astype(BF16)
    ms = (_dot(oo_hi, bd_bf16) + _dot(oo_lo, bd_bf16)) * (1.0 / HEAD_DIM)
    g = z_ref[:, 3 * w:4 * w]
    o_ref[...] = o * lax.rsqrt(ms + EPS) * gain_ref[...] * (g * jax.nn.sigmoid(g))


def _hgrn(zc, lb_logits, gain, layer):
    b, seq, _ = zc.shape
    sel, masks, blockdiag = _hgrn_constants()
    return pl.pallas_call(
        functools.partial(_hgrn_body, layer=layer),
        grid=(b, seq // HGRN_STEP),
        in_specs=[pl.BlockSpec((None, HGRN_STEP, GC_W), lambda i, j: (i, j, 0)),
                  _resident(lb_logits.shape), _resident(gain.shape),
                  _resident(sel.shape), _resident(masks.shape), _resident(blockdiag.shape)],
        out_specs=pl.BlockSpec((None, HGRN_STEP, GROUP_WIDTH), lambda i, j: (i, j, 0)),
        out_shape=jax.ShapeDtypeStruct((b, seq, GROUP_WIDTH), F32),
        scratch_shapes=[pltpu.VMEM((GROUP_WIDTH, GROUP_WIDTH), F32)],
        compiler_params=_params(("parallel", "arbitrary")),
        name="hgrn",
    )(zc, lb_logits, gain, jnp.asarray(sel), jnp.asarray(masks), jnp.asarray(blockdiag))


DIL_GROUP = 4


def _dilated_body(q_ref, k_ref, v_ref, bias_ref, o_ref, m_sc, l_sc, acc_sc):
    span = DIL_SPAN
    seq = q_ref.shape[0]
    lane = lax.broadcasted_iota(I32, (1, LANES), 1)
    head_lanes = [lane < HEAD_DIM, lane >= HEAD_DIM]
    a_i = lax.broadcasted_iota(I32, (span, 2 * span), 0)
    c_i = lax.broadcasted_iota(I32, (span, 2 * span), 1)
    delta = a_i + span - c_i
    window = (delta >= 0) & (delta <= span)
    half_ones = [jnp.where(lane < HEAD_DIM, 1.0, 0.0).astype(BF16) * jnp.ones((2 * span, 1), BF16),
                 jnp.where(lane >= HEAD_DIM, 1.0, 0.0).astype(BF16) * jnp.ones((2 * span, 1), BF16)]
    last = len(DILATED_BRANCHES) - 1

    for bi, (_, r) in enumerate(DILATED_BRANCHES):
        shift = r.bit_length() - 1

        def rows(start, r=r):
            return pl.ds(start, span) if r == 1 else pl.ds(start, span, stride=r)

        def blocks(i, carry, bi=bi, r=r, shift=shift, rows=rows):
            grp = range(DIL_GROUP)
            starts, q, kcat, vcat, mask = [], [], [], [], []
            for g in grp:
                blk = i * DIL_GROUP + g
                j = blk & (r - 1)
                n = lax.shift_right_logical(blk, shift)
                start = j + r * span * n
                prev = jnp.maximum(start - r * span, j)
                starts.append(start)
                q.append(q_ref[rows(start), :] * ATTN_SCALE)
                kcat.append(jnp.concatenate([k_ref[rows(prev), :], k_ref[rows(start), :]], axis=0).astype(BF16))
                vcat.append(jnp.concatenate([v_ref[rows(prev), :], v_ref[rows(start), :]], axis=0))
                mask.append(window & ((c_i >= span) | (n > 0)))
            mask = jnp.concatenate([jnp.where(mk, 0.0, NEG) for mk in mask], axis=0)
            num = [jnp.zeros((span, LANES), F32) for _ in grp]
            den = [jnp.zeros((span, LANES), F32) for _ in grp]
            m_b = jnp.zeros((DIL_GROUP * span, LANES), F32)
            for h in range(2):
                bias = bias_ref[bi, :, h * 2 * span:(h + 1) * 2 * span]
                logits = jnp.concatenate(
                    [_dot_nt(jnp.where(head_lanes[h], q[g], 0.0).astype(BF16), kcat[g]) + bias for g in grp], axis=0)
                logits = logits + mask
                m = jnp.max(logits, axis=1, keepdims=True)
                p = jnp.exp(logits - m).astype(BF16)
                m_b = jnp.where(head_lanes[h], m, m_b)
                for g in grp:
                    pg = p[g * span:(g + 1) * span]
                    num[g] = num[g] + _dot(pg, jnp.where(head_lanes[h], vcat[g], 0.0).astype(BF16))
                    den[g] = den[g] + _dot(pg, half_ones[h])
            num = jnp.concatenate(num, axis=0)
            den = jnp.concatenate(den, axis=0)
            if bi == 0:
                m_new, l_new, acc_new = m_b, den, num
            else:
                m_old = jnp.concatenate([m_sc[rows(st), :] for st in starts], axis=0)
                l_old = jnp.concatenate([l_sc[rows(st), :] for st in starts], axis=0)
                acc_old = jnp.concatenate([acc_sc[rows(st), :] for st in starts], axis=0)
                m_new = jnp.maximum(m_old, m_b)
                w_old = jnp.exp(m_old - m_new)
                w_b = jnp.exp(m_b - m_new)
                l_new = w_old * l_old + w_b * den
                acc_new = w_old * acc_old + w_b * num
            out = acc_new / l_new if bi == last else None
            for g, st in enumerate(starts):
                sl = slice(g * span, (g + 1) * span)
                if bi == last:
                    o_ref[rows(st), :] = out[sl]
                else:
                    m_sc[rows(st), :] = m_new[sl]
                    l_sc[rows(st), :] = l_new[sl]
                    acc_sc[rows(st), :] = acc_new[sl]
            return carry

        lax.fori_loop(0, seq // (span * DIL_GROUP), blocks, 0)


def _dilated(zd, bias):
    b, seq, _ = zd.shape
    pairs = GROUP_WIDTH // LANES
    assert seq % max(w for w, _ in DILATED_BRANCHES) == 0

    def spec(which):
        return pl.BlockSpec((None, seq, LANES), lambda i, j: (i, 0, pairs * which + j))

    nbr, span, bias_w = bias.shape
    return pl.pallas_call(
        _dilated_body,
        grid=(b, pairs),
        in_specs=[spec(0), spec(1), spec(2),
                  pl.BlockSpec((nbr, span, bias_w // pairs), lambda i, j: (0, 0, j))],
        out_specs=pl.BlockSpec((None, seq, LANES), lambda i, j: (i, 0, j)),
        out_shape=jax.ShapeDtypeStruct((b, seq, GROUP_WIDTH), F32),
        scratch_shapes=[pltpu.VMEM((seq, LANES), F32)] * 3,
        compiler_params=_params(("parallel", "parallel")),
        name="dilated",
    )(zd, zd, zd, bias)


def _out_ffn_body(*refs, final):
    x_ref, oat_ref, obt_ref, oc_ref, od_ref, wo_ref, g_ref, wg_ref, wu_ref, wd_ref = refs[0:10]
    gf_ref = refs[10] if final else None
    o_ref = refs[-1]
    gw = GROUP_WIDTH
    cd = jnp.concatenate([oc_ref[...], od_ref[...]], axis=1).astype(BF16)
    x = (x_ref[...] + _dot_tn(oat_ref[...].astype(BF16), wo_ref[0:gw, :])
         + _dot_tn(obt_ref[...].astype(BF16), wo_ref[gw:2 * gw, :]) + _dot(cd, wo_ref[2 * gw:4 * gw, :]))
    x = _swiglu_residual(x, g_ref[...], wg_ref, wu_ref, wd_ref)
    if final:
        x = _rms(x, gf_ref[...])
    o_ref[...] = x


def _out_ffn(x2d, o_at, o_bt, o_c, o_d, wo, gain, wg, wu, wd, gain_final, tm):
    m, d = x2d.shape
    seq = o_at.shape[2]
    per_b = seq // tm
    tile = lambda w: pl.BlockSpec((tm, w), lambda i: (i, 0))
    tile_t = pl.BlockSpec((None, GROUP_WIDTH, tm), lambda i: (i // per_b, 0, i % per_b))
    final = gain_final is not None
    args = [x2d, o_at, o_bt, o_c, o_d, wo, gain, wg, wu, wd] + ([gain_final] if final else [])
    in_specs = ([tile(d), tile_t, tile_t] + [tile(GROUP_WIDTH)] * 2
                + [_resident(wo.shape), _resident((1, d)), _resident(wg.shape), _resident(wu.shape),
                   _resident(wd.shape)] + ([_resident((1, d))] if final else []))
    return pl.pallas_call(
        functools.partial(_out_ffn_body, final=final),
        grid=(m // tm,),
        in_specs=in_specs,
        out_specs=tile(d),
        out_shape=jax.ShapeDtypeStruct((m, d), F32),
        compiler_params=_params(("parallel",)),
        name="out_ffn",
    )(*args)


def kernel(x, norm_ffn1, ffn1_gate, ffn1_up, ffn1_down, norm_mix, w_in, ckv_norm, w_kv_up, hgrn_lb_logits,
           hgrn_norm, w_out, norm_ffn2, ffn2_gate, ffn2_up, ffn2_down, rel_bias, norm_final):
    b, seq, d = x.shape
    depth = w_in.shape[0]
    nh = N_HEADS_GROUP
    m = b * seq
    tm = min(512, seq)
    topk = min(TOPK_MAX, seq // 4)

    toe_ids = _toeplitz_bucket_ids(seq)
    bias_a = _bias_tiles(rel_bias, toe_ids, 0)
    bias_b = _bias_tiles(rel_bias, toe_ids, nh)
    bias_d = _bias_tiles(rel_bias, _dilated_bucket_ids(), 2 * nh)

    x2d = x.reshape(m, d)
    for l in range(depth):
        x2d = _ffn(x2d, norm_ffn1[l][None], ffn1_gate[l].astype(BF16), ffn1_up[l].astype(BF16),
                   ffn1_down[l].astype(BF16), tm)
        za, zb, zc, zd = _inproj(x2d, norm_mix[l][None], _pack_w_in(w_in[l]), tm)
        za, zb, zc, zd = (z.reshape(b, seq, -1) for z in (za, zb, zc, zd))

        w_up = w_kv_up[l].reshape(KV_RANK, 2, nh, HEAD_DIM)
        wuk_t = jnp.transpose(w_up[:, 0], (1, 2, 0)).astype(BF16)
        wuv_t = jnp.transpose(w_up[:, 1], (1, 2, 0)).astype(BF16)
        o_at = _dsa(za, ckv_norm[l][None], wuk_t, wuv_t, bias_a, topk)
        o_bt = _moba(zb, bias_b)
        o_c = _hgrn(zc, hgrn_lb_logits, hgrn_norm[l][None], l).reshape(m, GROUP_WIDTH)
        o_d = _dilated(zd, bias_d).reshape(m, GROUP_WIDTH)

        gain_final = norm_final[None] if l == depth - 1 else None
        x2d = _out_ffn(x2d, o_at, o_bt, o_c, o_d, w_out[l].astype(BF16), norm_ffn2[l][None],
                       ffn2_gate[l].astype(BF16), ffn2_up[l].astype(BF16), ffn2_down[l].astype(BF16),
                       gain_final, tm)
    return x2d.reshape(b, seq, d)
```

```python
import functools
import math

import jax
import jax.numpy as jnp
import numpy as np
from jax import lax
from jax.experimental import pallas as pl
from jax.experimental.pallas import tpu as pltpu

F32 = jnp.float32
BF16 = jnp.bfloat16
I32 = jnp.int32

HEAD_DIM = 64
N_GROUPS = 4
GROUP_WIDTH = 256
N_HEADS_GROUP = GROUP_WIDTH // HEAD_DIM
ATTN_SCALE = HEAD_DIM ** -0.5
KV_RANK = 128
IDX_HEADS = 8
IDX_DIM = 32
TOPK_MAX = 256
MOBA_BLOCK = 256
MOBA_TOPK = 3
HGRN_CHUNK = 64
DILATED_BRANCHES = ((128, 1), (512, 4), (2048, 16))
DIL_SPAN = 128
N_BUCKETS = 32
MAX_DISTANCE = 2048
EPS = 1e-6

LANES = 128
BF16_EXACT_INT = 256
VMEM_LIMIT_BYTES = 56 * 1024 * 1024

NEG = -1e30
INT_MIN = -2 ** 31

GA_W, GB_W, GC_W, GD_W = 768, 768, 1024, 768
A_IW_LANE = 32


def _dot(a, b):
    return jnp.dot(a, b, preferred_element_type=F32)


def _dot_nt(a, b):
    return lax.dot_general(a, b, (((1,), (1,)), ((), ())), preferred_element_type=F32)


def _dot_tn(a, b):
    return lax.dot_general(a, b, (((0,), (0,)), ((), ())), preferred_element_type=F32)


def _rms(x, gain):
    return x * lax.rsqrt(jnp.mean(x * x, axis=-1, keepdims=True) + EPS) * gain


def _split3(x):
    hi = x.astype(BF16)
    r1 = x - hi.astype(F32)
    mid = r1.astype(BF16)
    lo = (r1 - mid.astype(F32)).astype(BF16)
    return hi, mid, lo


def _params(sem):
    return pltpu.CompilerParams(dimension_semantics=sem, vmem_limit_bytes=VMEM_LIMIT_BYTES)


def _resident(shape):
    nd = len(shape)
    return pl.BlockSpec(shape, lambda *_: (0,) * nd, pipeline_mode=pl.Buffered(1))


def _bucket_np(dist):
    max_exact = N_BUCKETS // 2
    n = np.maximum(dist, 0)
    nf = np.maximum(n, 1).astype(np.float64)
    large = max_exact + (np.log(nf / max_exact) / math.log(MAX_DISTANCE / max_exact)
                         * (N_BUCKETS - max_exact)).astype(np.int64)
    large = np.minimum(large, N_BUCKETS - 1)
    return np.where(n < max_exact, n, large).astype(np.int32)


def _toeplitz_bucket_ids(seq):
    nj = seq // LANES
    s = np.arange(LANES)[:, None]
    q = np.arange(LANES)[None, :]
    j = np.arange(nj)[:, None, None]
    return _bucket_np(LANES * j + q - s)


def _dilated_bucket_ids():
    a = np.arange(DIL_SPAN)[:, None]
    c = np.arange(2 * DIL_SPAN)[None, :]
    delta = a + DIL_SPAN - c
    return np.stack([_bucket_np(delta * r) for _, r in DILATED_BRANCHES])


def _bias_tiles_body(tab_ref, ids_ref, o_ref, *, head0):
    ids = ids_ref[0]
    c = ids.shape[1]
    for h in range(N_HEADS_GROUP):
        acc = jnp.zeros(ids.shape, F32)
        for k in range(N_BUCKETS):
            acc = jnp.where(ids == k, tab_ref[k, head0 + h], acc)
        o_ref[0, :, h * c:(h + 1) * c] = acc


def _bias_tiles(rel_bias, ids, head0):
    n, r, c = ids.shape
    return pl.pallas_call(
        functools.partial(_bias_tiles_body, head0=head0),
        grid=(n,),
        in_specs=[pl.BlockSpec(memory_space=pltpu.SMEM),
                  pl.BlockSpec((1, r, c), lambda i: (i, 0, 0))],
        out_specs=pl.BlockSpec((1, r, N_HEADS_GROUP * c), lambda i: (i, 0, 0)),
        out_shape=jax.ShapeDtypeStruct((n, r, N_HEADS_GROUP * c), F32),
        compiler_params=_params(("arbitrary",)),
        name="bias_tiles",
    )(rel_bias, jnp.asarray(ids))


FFN_CHUNK = 256


def _swiglu_residual(x, gain, wg_ref, wu_ref, wd_ref):
    h = _rms(x, gain).astype(BF16)
    d_ff = wg_ref.shape[1]
    acc = jnp.zeros(x.shape, F32)
    for c0 in range(0, d_ff, FFN_CHUNK):
        g = _dot(h, wg_ref[:, c0:c0 + FFN_CHUNK])
        u = _dot(h, wu_ref[:, c0:c0 + FFN_CHUNK])
        a = (g * jax.nn.sigmoid(g) * u).astype(BF16)
        acc = acc + _dot(a, wd_ref[c0:c0 + FFN_CHUNK, :])
    return x + 0.5 * acc


def _ffn_body(x_ref, g_ref, wg_ref, wu_ref, wd_ref, o_ref):
    o_ref[...] = _swiglu_residual(x_ref[...], g_ref[...], wg_ref, wu_ref, wd_ref)


def _ffn(x2d, gain, wg, wu, wd, tm):
    m, d = x2d.shape
    return pl.pallas_call(
        _ffn_body,
        grid=(m // tm,),
        in_specs=[pl.BlockSpec((tm, d), lambda i: (i, 0)),
                  _resident((1, d)), _resident(wg.shape), _resident(wu.shape), _resident(wd.shape)],
        out_specs=pl.BlockSpec((tm, d), lambda i: (i, 0)),
        out_shape=jax.ShapeDtypeStruct((m, d), F32),
        compiler_params=_params(("parallel",)),
        name="ffn",
    )(x2d, gain, wg, wu, wd)


def _inproj_body(x_ref, g_ref, w_ref, za_ref, zb_ref, zc_ref, zd_ref):
    h = _rms(x_ref[...], g_ref[...]).astype(BF16)
    c0 = 0
    for ref in (za_ref, zb_ref, zc_ref, zd_ref):
        w = ref.shape[1]
        ref[...] = _dot(h, w_ref[:, c0:c0 + w])
        c0 += w


def _inproj(x2d, gain, w_packed, tm):
    m, d = x2d.shape
    widths = (GA_W, GB_W, GC_W, GD_W)
    return pl.pallas_call(
        _inproj_body,
        grid=(m // tm,),
        in_specs=[pl.BlockSpec((tm, d), lambda i: (i, 0)), _resident((1, d)), _resident(w_packed.shape)],
        out_specs=[pl.BlockSpec((tm, w), lambda i: (i, 0)) for w in widths],
        out_shape=[jax.ShapeDtypeStruct((m, w), F32) for w in widths],
        compiler_params=_params(("parallel",)),
        name="inproj",
    )(x2d, gain, w_packed)


def _pack_w_in(w):
    d = w.shape[0]
    pad = jnp.zeros((d, GA_W - 680), w.dtype)
    return jnp.concatenate([w[:, 0:256], w[:, 384:640], w[:, 256:384], w[:, 640:680], pad, w[:, 680:]],
                           axis=1).astype(BF16)


def _online_softmax_step(s, vt_bf16, m, l, acc):
    m_new = jnp.maximum(m, jnp.max(s, axis=0, keepdims=True))
    alpha = jnp.exp(m - m_new)
    p = jnp.exp(s - m_new)
    l = alpha * l + jnp.sum(p, axis=0, keepdims=True)
    acc = alpha * acc + _dot(vt_bf16, p.astype(BF16))
    return m_new, l, acc


DSA_TQ = 128
DSA_TILE_PAD = 4
DSA_SCORE_GROUP = 4
DSA_COUNT_GROUP = 4
DSA_ATTN_GROUP = 4


def _dsa_body(q_ref, iq_ref, iwq_ref, ckv_ref, ikw_ref, gain_ref, wuk_ref, wuvt_ref, bias_ref,
              o_ref, c_sc, ct_sc, ik_sc, key_sc, dig_sc, work_sc, *, topk):
    tq = DSA_TQ
    n = pl.program_id(1)
    seq = ckv_ref.shape[0]
    nheads = N_HEADS_GROUP
    lane = lax.broadcasted_iota(I32, (1, LANES), 1)

    @pl.when(n == 0)
    def _():
        for r0 in range(0, seq, tq):
            c = _rms(ckv_ref[r0:r0 + tq, :], gain_ref[...])
            c_sc[r0:r0 + tq, :] = c.astype(BF16)
            ct_sc[:, r0:r0 + tq] = c.T.astype(BF16)
            ik = jnp.where(lane < IDX_DIM, ikw_ref[r0:r0 + tq, :], 0.0)
            ik4 = ik
            for rep in range(1, LANES // IDX_DIM):
                ik4 = ik4 + pltpu.roll(ik, rep * IDX_DIM, 1)
            ik_sc[r0:r0 + tq, :] = ik4.astype(BF16)

    iq = iq_ref[...]
    iw_t = (iwq_ref[...] * (IDX_HEADS * IDX_DIM) ** -0.5).T
    heads_per_group = LANES // IDX_DIM
    lhs = []
    for h in range(IDX_HEADS):
        blk = iq[:, LANES * (h // heads_per_group):LANES * (h // heads_per_group + 1)]
        lo = IDX_DIM * (h % heads_per_group)
        lhs.append(jnp.where((lane >= lo) & (lane < lo + IDX_DIM), blk, 0.0).astype(BF16))
    lhs = jnp.concatenate(lhs, axis=0)
    qpos = n * tq + lax.broadcasted_iota(I32, (1, tq), 1)
    srow = lax.broadcasted_iota(I32, (tq, 1), 0)
    pad_shift = DSA_TILE_PAD.bit_length() - 1
    ntiles = lax.shift_left(lax.shift_right_logical(n + DSA_TILE_PAD, pad_shift), pad_shift)

    def over_key_tiles(group, body, carry):
        ngroups = lax.shift_right_logical(ntiles, group.bit_length() - 1)
        return lax.fori_loop(0, ngroups, lambda i, c: body(i * group, group, c), carry)

    def score_tiles(kt, g, carry):
        k0 = pl.multiple_of(kt * tq, tq)
        s = _dot_nt(ik_sc[pl.ds(k0, g * tq), :], lhs)
        idx = jnp.zeros((g * tq, tq), F32)
        for h in range(IDX_HEADS):
            idx = idx + iw_t[A_IW_LANE + h:A_IW_LANE + h + 1, :] * jnp.maximum(s[:, h * tq:(h + 1) * tq], 0.0)
        bits = pltpu.bitcast(idx, I32)
        keys = bits ^ ((bits >> 31) & 0x7FFFFFFF)
        spos = k0 + lax.broadcasted_iota(I32, (g * tq, 1), 0)
        keys = jnp.where(spos <= qpos, keys, INT_MIN)
        key_sc[pl.ds(k0, g * tq), :] = keys
        u = keys ^ INT_MIN
        for d in range(4):
            digit = lax.shift_right_logical(u, 8 * d) & 0xFF
            dig_sc[d, pl.ds(k0, g * tq), :] = digit.astype(F32).astype(BF16)
        return carry

    over_key_tiles(DSA_SCORE_GROUP, score_tiles, 0)

    def count(pred):
        def body(kt, g, acc):
            for i in range(g):
                k0 = pl.multiple_of((kt + i) * tq, tq)
                acc = acc + jnp.where(pred(key_sc[pl.ds(k0, tq), :], k0), 1, 0)
            return acc
        acc = over_key_tiles(DSA_COUNT_GROUP, body, jnp.zeros((tq, tq), I32))
        return jnp.sum(acc, axis=0, keepdims=True)

    one, zero = jnp.ones((), BF16), jnp.zeros((), BF16)

    def count_digits_ge(tiles, cand):
        cand = cand.astype(BF16)

        def body(kt, g, acc):
            for i in range(g):
                k0 = pl.multiple_of((kt + i) * tq, tq)
                acc = acc + jnp.where(tiles(k0) >= cand, one, zero)
            return acc
        acc = over_key_tiles(DSA_COUNT_GROUP, body, jnp.zeros((tq, tq), BF16))
        return jnp.sum(acc.astype(F32), axis=0, keepdims=True)

    wanted = jnp.full((1, tq), topk, F32)
    t = jnp.zeros((1, tq), I32)
    for d in (3, 2, 1, 0):
        if d == 3:
            tiles = lambda k0: dig_sc[3, pl.ds(k0, tq), :]
        else:
            tiles = lambda k0: work_sc[pl.ds(k0, tq), :]

        def bit_iter(i, td, tiles=tiles, wanted=wanted):
            cand = td + lax.shift_right_logical(jnp.int32(128), i).astype(F32)
            return jnp.where(count_digits_ge(tiles, cand) >= wanted, cand, td)

        td = lax.fori_loop(0, 8, bit_iter, jnp.zeros((1, tq), F32))
        t = t | lax.shift_left(td.astype(I32), 8 * d)
        if d > 0:
            wanted = wanted - count_digits_ge(tiles, td + 1.0)
            td_b = td.astype(BF16)

            def narrow(kt, g, carry, d=d, tiles=tiles, td_b=td_b):
                for i in range(g):
                    k0 = pl.multiple_of((kt + i) * tq, tq)
                    work_sc[pl.ds(k0, tq), :] = jnp.where(tiles(k0) == td_b, dig_sc[d - 1, pl.ds(k0, tq), :], -one)
                return carry
            over_key_tiles(DSA_COUNT_GROUP, narrow, 0)
    tau = t ^ INT_MIN
    cnt_gt = count(lambda keys, k0: keys > tau)
    cnt_eq = count(lambda keys, k0: keys == tau)
    need = topk - cnt_gt
    tie_q = (cnt_eq > need) & (t != 0)

    @pl.when(jnp.max(tie_q.astype(I32)) > 0)
    def _():
        def pos_iter(i, p):
            cand = p | lax.shift_left(jnp.int32(1), (seq.bit_length() - 1) - i)
            cnt = count(lambda keys, k0: (keys == tau) & (k0 + srow < cand))
            return jnp.where(cnt < need, cand, p)
        p = lax.fori_loop(0, seq.bit_length(), pos_iter, jnp.zeros((1, tq), I32))
        p = jnp.where(tie_q, p, seq)

        def drop(kt, carry):
            k0 = pl.multiple_of(kt * tq, tq)
            keys = key_sc[pl.ds(k0, tq), :]
            key_sc[pl.ds(k0, tq), :] = jnp.where((keys == tau) & (k0 + srow > p), INT_MIN, keys)
            return carry
        lax.fori_loop(0, ntiles, drop, 0)

    tau_sel = jnp.maximum(tau, INT_MIN + 1)

    q = q_ref[...]
    qs = []
    for h in range(nheads):
        qh = q[:, h * HEAD_DIM:(h + 1) * HEAD_DIM].astype(BF16)
        qs.append((_dot(qh, wuk_ref[h]) * ATTN_SCALE).astype(BF16))
    qs = jnp.concatenate(qs, axis=0)
    cols = nheads * tq

    def attn_tiles(kt, g, carry):
        m, l, acc = carry
        k0 = pl.multiple_of(kt * tq, tq)
        s = _dot_nt(c_sc[pl.ds(k0, g * tq), :], qs)
        drop = jnp.where(key_sc[pl.ds(k0, g * tq), :] >= tau_sel, 0.0, NEG)
        bias = jnp.concatenate([bias_ref[jnp.maximum(n - kt - i, 0)] for i in range(g)], axis=0)
        s = s + bias + jnp.concatenate([drop] * nheads, axis=1)
        return _online_softmax_step(s, ct_sc[:, pl.ds(k0, g * tq)], m, l, acc)

    m0 = jnp.full((1, cols), NEG, F32)
    l0 = jnp.zeros((1, cols), F32)
    acc0 = jnp.zeros((KV_RANK, cols), F32)
    m, l, acc = over_key_tiles(DSA_ATTN_GROUP, attn_tiles, (m0, l0, acc0))
    ctx = (acc / l).astype(BF16)
    out_t = jnp.concatenate([_dot(wuvt_ref[h], ctx[:, h * tq:(h + 1) * tq]) for h in range(nheads)], axis=0)
    o_ref[...] = out_t.T


def _dsa(za, ckv_gain, wuk_t, wuv_t, bias, topk):
    b, seq, _ = za.shape
    tq = DSA_TQ
    assert seq % (tq * DSA_TILE_PAD) == 0
    assert seq // tq <= BF16_EXACT_INT
    return pl.pallas_call(
        functools.partial(_dsa_body, topk=topk),
        grid=(b, seq // tq),
        in_specs=[pl.BlockSpec((None, tq, 256), lambda i, j: (i, j, 0)),
                  pl.BlockSpec((None, tq, 256), lambda i, j: (i, j, 1)),
                  pl.BlockSpec((None, tq, LANES), lambda i, j: (i, j, 5)),
                  pl.BlockSpec((None, seq, LANES), lambda i, j: (i, 0, 4)),
                  pl.BlockSpec((None, seq, LANES), lambda i, j: (i, 0, 5)),
                  _resident(ckv_gain.shape), _resident(wuk_t.shape), _resident(wuv_t.shape),
                  _resident(bias.shape)],
        out_specs=pl.BlockSpec((None, tq, GROUP_WIDTH), lambda i, j: (i, j, 0)),
        out_shape=jax.ShapeDtypeStruct((b, seq, GROUP_WIDTH), F32),
        scratch_shapes=[pltpu.VMEM((seq, KV_RANK), BF16), pltpu.VMEM((KV_RANK, seq), BF16),
                        pltpu.VMEM((seq, LANES), BF16), pltpu.VMEM((seq, tq), I32),
                        pltpu.VMEM((4, seq, tq), BF16), pltpu.VMEM((seq, tq), BF16)],
        compiler_params=_params(("parallel", "arbitrary")),
        name="dsa",
    )(za, za, za, za, za, ckv_gain, wuk_t, wuv_t, bias)


MOBA_GROUP = 2


def _moba_bias_tile(bias_ref, d):
    lo = jnp.maximum(2 * d - 1, 0)
    t_same, t_next, t_prev = bias_ref[2 * d], bias_ref[2 * d + 1], bias_ref[lo]
    top, bot = [], []
    for h in range(N_HEADS_GROUP):
        hs = slice(h * LANES, (h + 1) * LANES)
        top += [t_same[:, hs], t_next[:, hs]]
        bot += [t_prev[:, hs], t_same[:, hs]]
    return jnp.concatenate([jnp.concatenate(top, axis=1), jnp.concatenate(bot, axis=1)], axis=0)


def _moba_body(q_ref, k_ref, v_ref, bias_ref, o_ref, kh_sc, vt_sc, km_sc, sel_sc, *, topb):
    blk = MOBA_BLOCK
    n = pl.program_id(1)
    seq = k_ref.shape[0]
    nb = seq // blk
    nheads = N_HEADS_GROUP

    @pl.when(n == 0)
    def _():
        for j in range(nb):
            kb = k_ref[j * blk:(j + 1) * blk, :]
            vbt = v_ref[j * blk:(j + 1) * blk, :].T
            km = jnp.mean(kb, axis=0, keepdims=True)
            for h in range(nheads):
                sl = slice(h * HEAD_DIM, (h + 1) * HEAD_DIM)
                kh_sc[h, j * blk:(j + 1) * blk, :] = kb[:, sl].astype(BF16)
                vt_sc[h, :, j * blk:(j + 1) * blk] = vbt[sl, :].astype(BF16)
                km_sc[h, j:j + 1, :] = km[:, sl]

    q = q_ref[...]
    rowid = lax.broadcasted_iota(I32, (nb, blk), 0)
    qs = []
    for h in range(nheads):
        qh = q[:, h * HEAD_DIM:(h + 1) * HEAD_DIM]
        km_hi, km_mid, _ = _split3(km_sc[h])
        q_hi, q_mid, _ = _split3(qh)
        gate = _dot_nt(km_hi, q_hi) + _dot_nt(km_mid, q_hi) + _dot_nt(km_hi, q_mid)
        gate = jnp.where(rowid < n, gate, -jnp.inf)
        rank = jnp.zeros((nb, blk), I32)
        for mth in range(nb):
            gm = gate[mth:mth + 1, :]
            beats = (gm > gate) | ((gm == gate) & (mth < rowid))
            rank = rank + jnp.where(beats, 1, 0)
        sel_sc[:, h * blk:(h + 1) * blk] = jnp.where((rank < topb) & (rowid < n), 0.0, NEG)
        qs.append((qh * ATTN_SCALE).astype(BF16))

    def logits(j0, d):
        s = jnp.concatenate([_dot_nt(kh_sc[h, pl.ds(j0, blk), :], qs[h]) for h in range(nheads)], axis=1)
        return s + _moba_bias_tile(bias_ref, d)

    def weighted_values(j0, p):
        p = p.astype(BF16)
        return jnp.concatenate([_dot(vt_sc[h, :, pl.ds(j0, blk)], p[:, h * blk:(h + 1) * blk])
                                for h in range(nheads)], axis=1)

    k0 = pl.multiple_of(n * blk, blk)
    wide = (blk, nheads * blk)
    causal = lax.broadcasted_iota(I32, wide, 0) <= (lax.broadcasted_iota(I32, wide, 1) & (blk - 1))
    s = jnp.where(causal, logits(k0, 0), NEG)
    m = jnp.max(s, axis=0, keepdims=True)
    p = jnp.exp(s - m)
    l = jnp.sum(p, axis=0, keepdims=True)
    acc = weighted_values(k0, p)

    def past_blocks(i, carry):
        m, l, acc = carry
        js = [i * MOBA_GROUP + g for g in range(MOBA_GROUP)]
        j0s = [pl.multiple_of(j * blk, blk) for j in js]
        s = jnp.concatenate([logits(j0, n - j) + sel_sc[pl.ds(j, 1), :] for j, j0 in zip(js, j0s)], axis=0)
        m_new = jnp.maximum(m, jnp.max(s, axis=0, keepdims=True))
        alpha = jnp.exp(m - m_new)
        p = jnp.exp(s - m_new)
        l = alpha * l + jnp.sum(p, axis=0, keepdims=True)
        acc = alpha * acc
        for g, j0 in enumerate(j0s):
            acc = acc + weighted_values(j0, p[g * blk:(g + 1) * blk])
        return m_new, l, acc

    ntrips = lax.shift_right_logical(n + MOBA_GROUP - 1, MOBA_GROUP.bit_length() - 1)
    m, l, acc = lax.fori_loop(0, ntrips, past_blocks, (m, l, acc))
    out = acc / l
    out_t = jnp.concatenate([out[:, h * blk:(h + 1) * blk] for h in range(nheads)], axis=0)
    o_ref[...] = out_t.T


def _moba(zb, bias):
    b, seq, _ = zb.shape
    blk = MOBA_BLOCK
    nb = seq // blk
    return pl.pallas_call(
        functools.partial(_moba_body, topb=min(MOBA_TOPK, nb)),
        grid=(b, nb),
        in_specs=[pl.BlockSpec((None, blk, GROUP_WIDTH), lambda i, j: (i, j, 0)),
                  pl.BlockSpec((None, seq, GROUP_WIDTH), lambda i, j: (i, 0, 1)),
                  pl.BlockSpec((None, seq, GROUP_WIDTH), lambda i, j: (i, 0, 2)),
                  _resident(bias.shape)],
        out_specs=pl.BlockSpec((None, blk, GROUP_WIDTH), lambda i, j: (i, j, 0)),
        out_shape=jax.ShapeDtypeStruct((b, seq, GROUP_WIDTH), F32),
        scratch_shapes=[pltpu.VMEM((N_HEADS_GROUP, seq, HEAD_DIM), BF16),
                        pltpu.VMEM((N_HEADS_GROUP, HEAD_DIM, seq), BF16),
                        pltpu.VMEM((N_HEADS_GROUP, nb, HEAD_DIM), F32),
                        pltpu.VMEM((nb, N_HEADS_GROUP * blk), F32)],
        compiler_params=_params(("parallel", "arbitrary")),
        name="moba",
    )(zb, zb, zb, bias)


HGRN_STEP = 256
_HGRN_LEVELS = (32, 16, 8, 4, 2, 1)


def _hgrn_constants():
    c = HGRN_CHUNK
    t = np.arange(c)[:, None]
    u = np.arange(c)[None, :]
    sel = [(u <= t),
           (u > t),
           np.ones((8, c), bool)]
    masks = []
    for m in _HGRN_LEVELS:
        r = (t // (2 * m)) * 2 * m + m - 1
        right = t > r
        sel.append(right & (u > r) & (u <= t))
        sel.append((~right) & (u > t) & (u <= r))
        s = u
        same_blk = (t // (2 * m)) == (s // (2 * m))
        masks.append(same_blk & right & (s <= r))
    masks.append(t == u)
    sel = np.concatenate(sel, axis=0).astype(np.float32)
    masks = np.stack([np.tile(mk, (HGRN_STEP // c, N_HEADS_GROUP)) for mk in masks]).astype(np.float32)
    hid = np.arange(GROUP_WIDTH) // HEAD_DIM
    blockdiag = (hid[:, None] == hid[None, :]).astype(np.float32)
    return sel, masks, blockdiag


def _hgrn_body(z_ref, lbl_ref, gain_ref, sel_ref, mask_ref, bd_ref, o_ref, st_sc, *, layer):
    c = HGRN_CHUNK
    w = GROUP_WIDTH
    nheads = N_HEADS_GROUP

    @pl.when(pl.program_id(1) == 0)
    def _():
        st_sc[...] = jnp.zeros(st_sc.shape, F32)

    logits = lbl_ref[...]
    e = jnp.exp(logits - jnp.max(logits, axis=0, keepdims=True))
    lb_w = e / jnp.sum(e, axis=0, keepdims=True)
    lb = jnp.sum(lb_w[0:layer + 1], axis=0, keepdims=True) - lb_w[0:1]

    bd = bd_ref[...] > 0.5
    bd_bf16 = bd_ref[...].astype(BF16)
    sel = sel_ref[...].astype(BF16)

    def blockdiag(x_bf16):
        return jnp.where(bd, jnp.concatenate([x_bf16] * nheads, axis=0), jnp.zeros((), BF16))

    chunks = [slice(r0, r0 + c) for r0 in range(0, z_ref.shape[0], c)]

    def per_chunk(fn):
        return jnp.concatenate([fn(rc) for rc in chunks], axis=0)

    q = z_ref[:, 0:w]
    f = lb + (1.0 - lb) * jax.nn.sigmoid(z_ref[:, w:2 * w])
    kin = 1.0 - f
    hi, mid, lo = _split3(jnp.log(f))
    decays = [jnp.exp(_dot(sel, hi[rc]) + _dot(sel, mid[rc]) + _dot(sel, lo[rc])) for rc in chunks]

    def decay_rows(base):
        return jnp.concatenate([d[base:base + c] for d in decays], axis=0)

    a = jnp.zeros(q.shape, F32)
    for li in range(len(_HGRN_LEVELS)):
        base = 2 * c + 8 + 2 * c * li
        qm = (q * decay_rows(base)).astype(BF16)
        km = (kin * decay_rows(base + c)).astype(BF16)
        a = a + mask_ref[li] * per_chunk(lambda rc: _dot_nt(qm[rc], blockdiag(km[rc])))
    q_bf16, kin_bf16 = q.astype(BF16), kin.astype(BF16)
    a = a + mask_ref[len(_HGRN_LEVELS)] * per_chunk(lambda rc: _dot_nt(q_bf16[rc], blockdiag(kin_bf16[rc])))

    a_bf16 = a.astype(BF16)
    v_bf16 = z_ref[:, 2 * w:3 * w].astype(BF16)
    o = per_chunk(lambda rc: _dot(a_bf16[rc], blockdiag(v_bf16[rc])))
    khat = (kin * decay_rows(c)).astype(BF16)
    deltas = [jnp.where(bd, _dot_tn(v_bf16[rc], khat[rc]), 0.0) for rc in chunks]
    q_dec = (q * decay_rows(0)).astype(BF16)
    st = st_sc[...]
    carried = []
    for ci, rc in enumerate(chunks):
        carried.append(_dot_nt(q_dec[rc], st.astype(BF16)))
        st = st * decays[ci][2 * c:2 * c + 1] + deltas[ci]
    st_sc[...] = st
    o = o + jnp.concatenate(carried, axis=0)

    oo = o * o
    oo_hi = oo.astype(BF16)
    oo_lo = (oo - oo_hi.astype(F32)).astype(BF16)
    ms = (_dot(oo_hi, bd_bf16) + _dot(oo_lo, bd_bf16)) * (1.0 / HEAD_DIM)
    g = z_ref[:, 3 * w:4 * w]
    o_ref[...] = o * lax.rsqrt(ms + EPS) * gain_ref[...] * (g * jax.nn.sigmoid(g))


def _hgrn(zc, lb_logits, gain, layer):
    b, seq, _ = zc.shape
    sel, masks, blockdiag = _hgrn_constants()
    return pl.pallas_call(
        functools.partial(_hgrn_body, layer=layer),
        grid=(b, seq // HGRN_STEP),
        in_specs=[pl.BlockSpec((None, HGRN_STEP, GC_W), lambda i, j: (i, j, 0)),
                  _resident(lb_logits.shape), _resident(gain.shape),
                  _resident(sel.shape), _resident(masks.shape), _resident(blockdiag.shape)],
        out_specs=pl.BlockSpec((None, HGRN_STEP, GROUP_WIDTH), lambda i, j: (i, j, 0)),
        out_shape=jax.ShapeDtypeStruct((b, seq, GROUP_WIDTH), F32),
        scratch_shapes=[pltpu.VMEM((GROUP_WIDTH, GROUP_WIDTH), F32)],
        compiler_params=_params(("parallel", "arbitrary")),
        name="hgrn",
    )(zc, lb_logits, gain, jnp.asarray(sel), jnp.asarray(masks), jnp.asarray(blockdiag))


DIL_GROUP = 4


def _dilated_body(q_ref, k_ref, v_ref, bias_ref, o_ref, m_sc, l_sc, acc_sc):
    span = DIL_SPAN
    seq = q_ref.shape[0]
    lane = lax.broadcasted_iota(I32, (1, LANES), 1)
    head_lanes = [lane < HEAD_DIM, lane >= HEAD_DIM]
    a_i = lax.broadcasted_iota(I32, (span, 2 * span), 0)
    c_i = lax.broadcasted_iota(I32, (span, 2 * span), 1)
    delta = a_i + span - c_i
    window = (delta >= 0) & (delta <= span)
    half_ones = [jnp.where(lane < HEAD_DIM, 1.0, 0.0).astype(BF16) * jnp.ones((2 * span, 1), BF16),
                 jnp.where(lane >= HEAD_DIM, 1.0, 0.0).astype(BF16) * jnp.ones((2 * span, 1), BF16)]
    last = len(DILATED_BRANCHES) - 1

    for bi, (_, r) in enumerate(DILATED_BRANCHES):
        shift = r.bit_length() - 1

        def rows(start, r=r):
            return pl.ds(start, span) if r == 1 else pl.ds(start, span, stride=r)

        def blocks(i, carry, bi=bi, r=r, shift=shift, rows=rows):
            grp = range(DIL_GROUP)
            starts, q, kcat, vcat, mask = [], [], [], [], []
            for g in grp:
                blk = i * DIL_GROUP + g
                j = blk & (r - 1)
                n = lax.shift_right_logical(blk, shift)
                start = j + r * span * n
                prev = jnp.maximum(start - r * span, j)
                starts.append(start)
                q.append(q_ref[rows(start), :] * ATTN_SCALE)
                kcat.append(jnp.concatenate([k_ref[rows(prev), :], k_ref[rows(start), :]], axis=0).astype(BF16))
                vcat.append(jnp.concatenate([v_ref[rows(prev), :], v_ref[rows(start), :]], axis=0))
                mask.append(window & ((c_i >= span) | (n > 0)))
            mask = jnp.concatenate([jnp.where(mk, 0.0, NEG) for mk in mask], axis=0)
            num = [jnp.zeros((span, LANES), F32) for _ in grp]
            den = [jnp.zeros((span, LANES), F32) for _ in grp]
            m_b = jnp.zeros((DIL_GROUP * span, LANES), F32)
            for h in range(2):
                bias = bias_ref[bi, :, h * 2 * span:(h + 1) * 2 * span]
                logits = jnp.concatenate(
                    [_dot_nt(jnp.where(head_lanes[h], q[g], 0.0).astype(BF16), kcat[g]) + bias for g in grp], axis=0)
                logits = logits + mask
                m = jnp.max(logits, axis=1, keepdims=True)
                p = jnp.exp(logits - m).astype(BF16)
                m_b = jnp.where(head_lanes[h], m, m_b)
                for g in grp:
                    pg = p[g * span:(g + 1) * span]
                    num[g] = num[g] + _dot(pg, jnp.where(head_lanes[h], vcat[g], 0.0).astype(BF16))
                    den[g] = den[g] + _dot(pg, half_ones[h])
            num = jnp.concatenate(num, axis=0)
            den = jnp.concatenate(den, axis=0)
            if bi == 0:
                m_new, l_new, acc_new = m_b, den, num
            else:
                m_old = jnp.concatenate([m_sc[rows(st), :] for st in starts], axis=0)
                l_old = jnp.concatenate([l_sc[rows(st), :] for st in starts], axis=0)
                acc_old = jnp.concatenate([acc_sc[rows(st), :] for st in starts], axis=0)
                m_new = jnp.maximum(m_old, m_b)
                w_old = jnp.exp(m_old - m_new)
                w_b = jnp.exp(m_b - m_new)
                l_new = w_old * l_old + w_b * den
                acc_new = w_old * acc_old + w_b * num
            out = acc_new / l_new if bi == last else None
            for g, st in enumerate(starts):
                sl = slice(g * span, (g + 1) * span)
                if bi == last:
                    o_ref[rows(st), :] = out[sl]
                else:
                    m_sc[rows(st), :] = m_new[sl]
                    l_sc[rows(st), :] = l_new[sl]
                    acc_sc[rows(st), :] = acc_new[sl]
            return carry

        lax.fori_loop(0, seq // (span * DIL_GROUP), blocks, 0)


def _dilated(zd, bias):
    b, seq, _ = zd.shape
    pairs = GROUP_WIDTH // LANES
    assert seq % max(w for w, _ in DILATED_BRANCHES) == 0

    def spec(which):
        return pl.BlockSpec((None, seq, LANES), lambda i, j: (i, 0, pairs * which + j))

    nbr, span, bias_w = bias.shape
    return pl.pallas_call(
        _dilated_body,
        grid=(b, pairs),
        in_specs=[spec(0), spec(1), spec(2),
                  pl.BlockSpec((nbr, span, bias_w // pairs), lambda i, j: (0, 0, j))],
        out_specs=pl.BlockSpec((None, seq, LANES), lambda i, j: (i, 0, j)),
        out_shape=jax.ShapeDtypeStruct((b, seq, GROUP_WIDTH), F32),
        scratch_shapes=[pltpu.VMEM((seq, LANES), F32)] * 3,
        compiler_params=_params(("parallel", "parallel")),
        name="dilated",
    )(zd, zd, zd, bias)


def _out_ffn_body(*refs, final):
    x_ref, oa_ref, ob_ref, oc_ref, od_ref, wo_ref, g_ref, wg_ref, wu_ref, wd_ref = refs[0:10]
    gf_ref = refs[10] if final else None
    o_ref = refs[-1]
    cat = jnp.concatenate([oa_ref[...], ob_ref[...], oc_ref[...], od_ref[...]], axis=1).astype(BF16)
    x = x_ref[...] + _dot(cat, wo_ref[...])
    x = _swiglu_residual(x, g_ref[...], wg_ref, wu_ref, wd_ref)
    if final:
        x = _rms(x, gf_ref[...])
    o_ref[...] = x


def _out_ffn(x2d, mixers, wo, gain, wg, wu, wd, gain_final, tm):
    m, d = x2d.shape
    tile = lambda w: pl.BlockSpec((tm, w), lambda i: (i, 0))
    final = gain_final is not None
    args = [x2d, *mixers, wo, gain, wg, wu, wd] + ([gain_final] if final else [])
    in_specs = ([tile(d)] + [tile(GROUP_WIDTH)] * 4
                + [_resident(wo.shape), _resident((1, d)), _resident(wg.shape), _resident(wu.shape),
                   _resident(wd.shape)] + ([_resident((1, d))] if final else []))
    return pl.pallas_call(
        functools.partial(_out_ffn_body, final=final),
        grid=(m // tm,),
        in_specs=in_specs,
        out_specs=tile(d),
        out_shape=jax.ShapeDtypeStruct((m, d), F32),
        compiler_params=_params(("parallel",)),
        name="out_ffn",
    )(*args)


def kernel(x, norm_ffn1, ffn1_gate, ffn1_up, ffn1_down, norm_mix, w_in, ckv_norm, w_kv_up, hgrn_lb_logits,
           hgrn_norm, w_out, norm_ffn2, ffn2_gate, ffn2_up, ffn2_down, rel_bias, norm_final):
    b, seq, d = x.shape
    depth = w_in.shape[0]
    nh = N_HEADS_GROUP
    m = b * seq
    tm = min(512, seq)
    topk = min(TOPK_MAX, seq // 4)

    toe_ids = _toeplitz_bucket_ids(seq)
    bias_a = _bias_tiles(rel_bias, toe_ids, 0)
    bias_b = _bias_tiles(rel_bias, toe_ids, nh)
    bias_d = _bias_tiles(rel_bias, _dilated_bucket_ids(), 2 * nh)

    x2d = x.reshape(m, d)
    for l in range(depth):
        x2d = _ffn(x2d, norm_ffn1[l][None], ffn1_gate[l].astype(BF16), ffn1_up[l].astype(BF16),
                   ffn1_down[l].astype(BF16), tm)
        za, zb, zc, zd = _inproj(x2d, norm_mix[l][None], _pack_w_in(w_in[l]), tm)
        za, zb, zc, zd = (z.reshape(b, seq, -1) for z in (za, zb, zc, zd))

        w_up = w_kv_up[l].reshape(KV_RANK, 2, nh, HEAD_DIM)
        wuk_t = jnp.transpose(w_up[:, 0], (1, 2, 0)).astype(BF16)
        wuv_t = jnp.transpose(w_up[:, 1], (1, 2, 0)).astype(BF16)
        mixers = [_dsa(za, ckv_norm[l][None], wuk_t, wuv_t, bias_a, topk), _moba(zb, bias_b),
                  _hgrn(zc, hgrn_lb_logits, hgrn_norm[l][None], l), _dilated(zd, bias_d)]
        mixers = [o.reshape(m, GROUP_WIDTH) for o in mixers]

        gain_final = norm_final[None] if l == depth - 1 else None
        x2d = _out_ffn(x2d, mixers, w_out[l].astype(BF16), norm_ffn2[l][None],
                       ffn2_gate[l].astype(BF16), ffn2_up[l].astype(BF16), ffn2_down[l].astype(BF16),
                       gain_final, tm)
    return x2d.reshape(b, seq, d)
```

```python
import functools
import math

import jax
import jax.numpy as jnp
import numpy as np
from jax import lax
from jax.experimental import pallas as pl
from jax.experimental.pallas import tpu as pltpu

F32 = jnp.float32
BF16 = jnp.bfloat16
I32 = jnp.int32

HEAD_DIM = 64
N_GROUPS = 4
GROUP_WIDTH = 256
N_HEADS_GROUP = GROUP_WIDTH // HEAD_DIM
ATTN_SCALE = HEAD_DIM ** -0.5
KV_RANK = 128
IDX_HEADS = 8
IDX_DIM = 32
TOPK_MAX = 256
MOBA_BLOCK = 256
MOBA_TOPK = 3
HGRN_CHUNK = 64
DILATED_BRANCHES = ((128, 1), (512, 4), (2048, 16))
DIL_SPAN = 128
N_BUCKETS = 32
MAX_DISTANCE = 2048
EPS = 1e-6

LANES = 128
VMEM_LIMIT_BYTES = 56 * 1024 * 1024

NEG = -1e30
INT_MIN = -2 ** 31

GA_W, GB_W, GC_W, GD_W = 768, 768, 1024, 768
A_IW_LANE = 32


def _dot(a, b):
    return jnp.dot(a, b, preferred_element_type=F32)


def _dot_nt(a, b):
    return lax.dot_general(a, b, (((1,), (1,)), ((), ())), preferred_element_type=F32)


def _dot_tn(a, b):
    return lax.dot_general(a, b, (((0,), (0,)), ((), ())), preferred_element_type=F32)


def _rms(x, gain):
    return x * lax.rsqrt(jnp.mean(x * x, axis=-1, keepdims=True) + EPS) * gain


def _split3(x):
    hi = x.astype(BF16)
    r1 = x - hi.astype(F32)
    mid = r1.astype(BF16)
    lo = (r1 - mid.astype(F32)).astype(BF16)
    return hi, mid, lo


def _params(sem):
    return pltpu.CompilerParams(dimension_semantics=sem, vmem_limit_bytes=VMEM_LIMIT_BYTES)


def _resident(shape):
    nd = len(shape)
    return pl.BlockSpec(shape, lambda *_: (0,) * nd, pipeline_mode=pl.Buffered(1))


def _bucket_np(dist):
    max_exact = N_BUCKETS // 2
    n = np.maximum(dist, 0)
    nf = np.maximum(n, 1).astype(np.float64)
    large = max_exact + (np.log(nf / max_exact) / math.log(MAX_DISTANCE / max_exact)
                         * (N_BUCKETS - max_exact)).astype(np.int64)
    large = np.minimum(large, N_BUCKETS - 1)
    return np.where(n < max_exact, n, large).astype(np.int32)


def _toeplitz_bucket_ids(seq):
    nj = seq // LANES
    s = np.arange(LANES)[:, None]
    q = np.arange(LANES)[None, :]
    j = np.arange(nj)[:, None, None]
    return _bucket_np(LANES * j + q - s)


def _dilated_bucket_ids():
    a = np.arange(DIL_SPAN)[:, None]
    c = np.arange(2 * DIL_SPAN)[None, :]
    delta = a + DIL_SPAN - c
    return np.stack([_bucket_np(delta * r) for _, r in DILATED_BRANCHES])


def _bias_tiles_body(tab_ref, ids_ref, o_ref, *, head0):
    ids = ids_ref[0]
    c = ids.shape[1]
    for h in range(N_HEADS_GROUP):
        acc = jnp.zeros(ids.shape, F32)
        for k in range(N_BUCKETS):
            acc = jnp.where(ids == k, tab_ref[k, head0 + h], acc)
        o_ref[0, :, h * c:(h + 1) * c] = acc


def _bias_tiles(rel_bias, ids, head0):
    n, r, c = ids.shape
    return pl.pallas_call(
        functools.partial(_bias_tiles_body, head0=head0),
        grid=(n,),
        in_specs=[pl.BlockSpec(memory_space=pltpu.SMEM),
                  pl.BlockSpec((1, r, c), lambda i: (i, 0, 0))],
        out_specs=pl.BlockSpec((1, r, N_HEADS_GROUP * c), lambda i: (i, 0, 0)),
        out_shape=jax.ShapeDtypeStruct((n, r, N_HEADS_GROUP * c), F32),
        compiler_params=_params(("arbitrary",)),
        name="bias_tiles",
    )(rel_bias, jnp.asarray(ids))


FFN_CHUNK = 256


def _swiglu_residual(x, gain, wg_ref, wu_ref, wd_ref):
    h = _rms(x, gain).astype(BF16)
    d_ff = wg_ref.shape[1]
    acc = jnp.zeros(x.shape, F32)
    for c0 in range(0, d_ff, FFN_CHUNK):
        g = _dot(h, wg_ref[:, c0:c0 + FFN_CHUNK])
        u = _dot(h, wu_ref[:, c0:c0 + FFN_CHUNK])
        a = (g * jax.nn.sigmoid(g) * u).astype(BF16)
        acc = acc + _dot(a, wd_ref[c0:c0 + FFN_CHUNK, :])
    return x + 0.5 * acc


def _ffn_body(x_ref, g_ref, wg_ref, wu_ref, wd_ref, o_ref):
    o_ref[...] = _swiglu_residual(x_ref[...], g_ref[...], wg_ref, wu_ref, wd_ref)


def _ffn(x2d, gain, wg, wu, wd, tm):
    m, d = x2d.shape
    return pl.pallas_call(
        _ffn_body,
        grid=(m // tm,),
        in_specs=[pl.BlockSpec((tm, d), lambda i: (i, 0)),
                  _resident((1, d)), _resident(wg.shape), _resident(wu.shape), _resident(wd.shape)],
        out_specs=pl.BlockSpec((tm, d), lambda i: (i, 0)),
        out_shape=jax.ShapeDtypeStruct((m, d), F32),
        compiler_params=_params(("parallel",)),
        name="ffn",
    )(x2d, gain, wg, wu, wd)


def _inproj_body(x_ref, g_ref, w_ref, za_ref, zb_ref, zc_ref, zd_ref):
    h = _rms(x_ref[...], g_ref[...]).astype(BF16)
    c0 = 0
    for ref in (za_ref, zb_ref, zc_ref, zd_ref):
        w = ref.shape[1]
        ref[...] = _dot(h, w_ref[:, c0:c0 + w])
        c0 += w


def _inproj(x2d, gain, w_packed, tm):
    m, d = x2d.shape
    widths = (GA_W, GB_W, GC_W, GD_W)
    return pl.pallas_call(
        _inproj_body,
        grid=(m // tm,),
        in_specs=[pl.BlockSpec((tm, d), lambda i: (i, 0)), _resident((1, d)), _resident(w_packed.shape)],
        out_specs=[pl.BlockSpec((tm, w), lambda i: (i, 0)) for w in widths],
        out_shape=[jax.ShapeDtypeStruct((m, w), F32) for w in widths],
        compiler_params=_params(("parallel",)),
        name="inproj",
    )(x2d, gain, w_packed)


def _pack_w_in(w):
    d = w.shape[0]
    pad = jnp.zeros((d, GA_W - 680), w.dtype)
    return jnp.concatenate([w[:, 0:256], w[:, 384:640], w[:, 256:384], w[:, 640:680], pad, w[:, 680:]],
                           axis=1).astype(BF16)


def _online_softmax_step(s, vt_bf16, m, l, acc):
    m_new = jnp.maximum(m, jnp.max(s, axis=0, keepdims=True))
    alpha = jnp.exp(m - m_new)
    p = jnp.exp(s - m_new)
    l = alpha * l + jnp.sum(p, axis=0, keepdims=True)
    acc = alpha * acc + _dot(vt_bf16, p.astype(BF16))
    return m_new, l, acc


DSA_TQ = 256
DSA_TK = LANES
DSA_TILE_PAD = 4
DSA_SCORE_GROUP = 2
DSA_COUNT_GROUP = 4
DSA_ATTN_GROUP = 4


def _dsa_body(q_ref, iq_ref, iwq_ref, ckv_ref, ikw_ref, gain_ref, wuk_ref, wuvt_ref, bias_ref,
              o_ref, c_sc, ct_sc, ik_sc, key_sc, *, topk):
    tq, tk = DSA_TQ, DSA_TK
    n = pl.program_id(1)
    seq = ckv_ref.shape[0]
    nheads = N_HEADS_GROUP
    lane = lax.broadcasted_iota(I32, (1, LANES), 1)

    @pl.when(n == 0)
    def _():
        for r0 in range(0, seq, tk):
            c = _rms(ckv_ref[r0:r0 + tk, :], gain_ref[...])
            c_sc[r0:r0 + tk, :] = c.astype(BF16)
            ct_sc[:, r0:r0 + tk] = c.T.astype(BF16)
            ik = jnp.where(lane < IDX_DIM, ikw_ref[r0:r0 + tk, :], 0.0)
            ik4 = ik
            for rep in range(1, LANES // IDX_DIM):
                ik4 = ik4 + pltpu.roll(ik, rep * IDX_DIM, 1)
            ik_sc[r0:r0 + tk, :] = ik4.astype(BF16)

    iq = iq_ref[...]
    iw_t = (iwq_ref[...] * (IDX_HEADS * IDX_DIM) ** -0.5).T
    tiles_per_q = tq // tk
    heads_per_group = LANES // IDX_DIM
    lhs = []
    for h in range(IDX_HEADS):
        blk = iq[:, LANES * (h // heads_per_group):LANES * (h // heads_per_group + 1)]
        lo = IDX_DIM * (h % heads_per_group)
        lhs.append(jnp.where((lane >= lo) & (lane < lo + IDX_DIM), blk, 0.0).astype(BF16))
    lhs = jnp.concatenate(lhs, axis=0)
    qpos = n * tq + lax.broadcasted_iota(I32, (1, tq), 1)
    srow = lax.broadcasted_iota(I32, (tk, 1), 0)
    pad_shift = DSA_TILE_PAD.bit_length() - 1
    ntiles = lax.shift_left(
        lax.shift_right_logical((n + 1) * tiles_per_q + DSA_TILE_PAD - 1, pad_shift), pad_shift)

    def over_key_tiles(group, body, carry):
        ngroups = lax.shift_right_logical(ntiles, group.bit_length() - 1)
        return lax.fori_loop(0, ngroups, lambda i, c: body(i * group, group, c), carry)

    def score_tiles(kt, g, carry):
        k0 = pl.multiple_of(kt * tk, tk)
        s = _dot_nt(ik_sc[pl.ds(k0, g * tk), :], lhs)
        idx = jnp.zeros((g * tk, tq), F32)
        for h in range(IDX_HEADS):
            idx = idx + iw_t[A_IW_LANE + h:A_IW_LANE + h + 1, :] * jnp.maximum(s[:, h * tq:(h + 1) * tq], 0.0)
        bits = pltpu.bitcast(idx, I32)
        keys = bits ^ ((bits >> 31) & 0x7FFFFFFF)
        spos = k0 + lax.broadcasted_iota(I32, (g * tk, 1), 0)
        key_sc[pl.ds(k0, g * tk), :] = jnp.where(spos <= qpos, keys, INT_MIN)
        return carry

    over_key_tiles(DSA_SCORE_GROUP, score_tiles, 0)

    def count(pred):
        def body(kt, g, acc):
            for i in range(g):
                k0 = pl.multiple_of((kt + i) * tk, tk)
                acc = acc + jnp.where(pred(key_sc[pl.ds(k0, tk), :], k0), 1, 0)
            return acc
        acc = over_key_tiles(DSA_COUNT_GROUP, body, jnp.zeros((tk, tq), I32))
        return jnp.sum(acc, axis=0, keepdims=True)

    def bit_iter(i, t):
        cand = t | lax.shift_left(jnp.int32(1), 31 - i)
        cand_s = cand ^ INT_MIN
        cnt = count(lambda keys, k0: keys >= cand_s)
        return jnp.where(cnt >= topk, cand, t)

    t = lax.fori_loop(0, 32, bit_iter, jnp.zeros((1, tq), I32))
    tau = t ^ INT_MIN
    cnt_gt = count(lambda keys, k0: keys > tau)
    cnt_eq = count(lambda keys, k0: keys == tau)
    need = topk - cnt_gt
    tie_q = (cnt_eq > need) & (t != 0)

    @pl.when(jnp.max(tie_q.astype(I32)) > 0)
    def _():
        def pos_iter(i, p):
            cand = p | lax.shift_left(jnp.int32(1), (seq.bit_length() - 1) - i)
            cnt = count(lambda keys, k0: (keys == tau) & (k0 + srow < cand))
            return jnp.where(cnt < need, cand, p)
        p = lax.fori_loop(0, seq.bit_length(), pos_iter, jnp.zeros((1, tq), I32))
        p = jnp.where(tie_q, p, seq)

        def drop(kt, carry):
            k0 = pl.multiple_of(kt * tk, tk)
            keys = key_sc[pl.ds(k0, tk), :]
            key_sc[pl.ds(k0, tk), :] = jnp.where((keys == tau) & (k0 + srow > p), INT_MIN, keys)
            return carry
        lax.fori_loop(0, ntiles, drop, 0)

    tau_sel = jnp.maximum(tau, INT_MIN + 1)

    q = q_ref[...]
    qs = []
    for h in range(nheads):
        qh = q[:, h * HEAD_DIM:(h + 1) * HEAD_DIM].astype(BF16)
        qs.append((_dot(qh, wuk_ref[h]) * ATTN_SCALE).astype(BF16))
    qs = jnp.concatenate(qs, axis=0)
    cols = nheads * tq

    def bias_rows(kt):
        parts = [bias_ref[jnp.maximum(n * tiles_per_q + qa - kt, 0)] for qa in range(tiles_per_q)]
        return jnp.concatenate([p[:, h * LANES:(h + 1) * LANES] for h in range(nheads) for p in parts], axis=1)

    def attn_tiles(kt, g, carry):
        m, l, acc = carry
        k0 = pl.multiple_of(kt * tk, tk)
        s = _dot_nt(c_sc[pl.ds(k0, g * tk), :], qs)
        drop = jnp.where(key_sc[pl.ds(k0, g * tk), :] >= tau_sel, 0.0, NEG)
        bias = jnp.concatenate([bias_rows(kt + i) for i in range(g)], axis=0)
        s = s + bias + jnp.concatenate([drop] * nheads, axis=1)
        return _online_softmax_step(s, ct_sc[:, pl.ds(k0, g * tk)], m, l, acc)

    m0 = jnp.full((1, cols), NEG, F32)
    l0 = jnp.zeros((1, cols), F32)
    acc0 = jnp.zeros((KV_RANK, cols), F32)
    m, l, acc = over_key_tiles(DSA_ATTN_GROUP, attn_tiles, (m0, l0, acc0))
    ctx = (acc / l).astype(BF16)
    out_t = jnp.concatenate([_dot(wuvt_ref[h], ctx[:, h * tq:(h + 1) * tq]) for h in range(nheads)], axis=0)
    o_ref[...] = out_t.T


def _dsa(za, ckv_gain, wuk_t, wuv_t, bias, topk):
    b, seq, _ = za.shape
    tq = DSA_TQ
    assert seq % tq == 0 and seq % (DSA_TK * DSA_TILE_PAD) == 0
    return pl.pallas_call(
        functools.partial(_dsa_body, topk=topk),
        grid=(b, seq // tq),
        in_specs=[pl.BlockSpec((None, tq, 256), lambda i, j: (i, j, 0)),
                  pl.BlockSpec((None, tq, 256), lambda i, j: (i, j, 1)),
                  pl.BlockSpec((None, tq, LANES), lambda i, j: (i, j, 5)),
                  pl.BlockSpec((None, seq, LANES), lambda i, j: (i, 0, 4)),
                  pl.BlockSpec((None, seq, LANES), lambda i, j: (i, 0, 5)),
                  _resident(ckv_gain.shape), _resident(wuk_t.shape), _resident(wuv_t.shape),
                  _resident(bias.shape)],
        out_specs=pl.BlockSpec((None, tq, GROUP_WIDTH), lambda i, j: (i, j, 0)),
        out_shape=jax.ShapeDtypeStruct((b, seq, GROUP_WIDTH), F32),
        scratch_shapes=[pltpu.VMEM((seq, KV_RANK), BF16), pltpu.VMEM((KV_RANK, seq), BF16),
                        pltpu.VMEM((seq, LANES), BF16), pltpu.VMEM((seq, tq), I32)],
        compiler_params=_params(("parallel", "arbitrary")),
        name="dsa",
    )(za, za, za, za, za, ckv_gain, wuk_t, wuv_t, bias)


MOBA_GROUP = 2


def _moba_bias_tile(bias_ref, d):
    lo = jnp.maximum(2 * d - 1, 0)
    t_same, t_next, t_prev = bias_ref[2 * d], bias_ref[2 * d + 1], bias_ref[lo]
    top, bot = [], []
    for h in range(N_HEADS_GROUP):
        hs = slice(h * LANES, (h + 1) * LANES)
        top += [t_same[:, hs], t_next[:, hs]]
        bot += [t_prev[:, hs], t_same[:, hs]]
    return jnp.concatenate([jnp.concatenate(top, axis=1), jnp.concatenate(bot, axis=1)], axis=0)


def _moba_body(q_ref, k_ref, v_ref, bias_ref, o_ref, kh_sc, vt_sc, km_sc, sel_sc, *, topb):
    blk = MOBA_BLOCK
    n = pl.program_id(1)
    seq = k_ref.shape[0]
    nb = seq // blk
    nheads = N_HEADS_GROUP

    @pl.when(n == 0)
    def _():
        for j in range(nb):
            kb = k_ref[j * blk:(j + 1) * blk, :]
            vbt = v_ref[j * blk:(j + 1) * blk, :].T
            km = jnp.mean(kb, axis=0, keepdims=True)
            for h in range(nheads):
                sl = slice(h * HEAD_DIM, (h + 1) * HEAD_DIM)
                kh_sc[h, j * blk:(j + 1) * blk, :] = kb[:, sl].astype(BF16)
                vt_sc[h, :, j * blk:(j + 1) * blk] = vbt[sl, :].astype(BF16)
                km_sc[h, j:j + 1, :] = km[:, sl]

    q = q_ref[...]
    rowid = lax.broadcasted_iota(I32, (nb, blk), 0)
    qs = []
    for h in range(nheads):
        qh = q[:, h * HEAD_DIM:(h + 1) * HEAD_DIM]
        km_hi, km_mid, _ = _split3(km_sc[h])
        q_hi, q_mid, _ = _split3(qh)
        gate = _dot_nt(km_hi, q_hi) + _dot_nt(km_mid, q_hi) + _dot_nt(km_hi, q_mid)
        gate = jnp.where(rowid < n, gate, -jnp.inf)
        rank = jnp.zeros((nb, blk), I32)
        for mth in range(nb):
            gm = gate[mth:mth + 1, :]
            beats = (gm > gate) | ((gm == gate) & (mth < rowid))
            rank = rank + jnp.where(beats, 1, 0)
        sel_sc[:, h * blk:(h + 1) * blk] = jnp.where((rank < topb) & (rowid < n), 0.0, NEG)
        qs.append((qh * ATTN_SCALE).astype(BF16))

    def logits(j0, d):
        s = jnp.concatenate([_dot_nt(kh_sc[h, pl.ds(j0, blk), :], qs[h]) for h in range(nheads)], axis=1)
        return s + _moba_bias_tile(bias_ref, d)

    def weighted_values(j0, p):
        p = p.astype(BF16)
        return jnp.concatenate([_dot(vt_sc[h, :, pl.ds(j0, blk)], p[:, h * blk:(h + 1) * blk])
                                for h in range(nheads)], axis=1)

    k0 = pl.multiple_of(n * blk, blk)
    wide = (blk, nheads * blk)
    causal = lax.broadcasted_iota(I32, wide, 0) <= (lax.broadcasted_iota(I32, wide, 1) & (blk - 1))
    s = jnp.where(causal, logits(k0, 0), NEG)
    m = jnp.max(s, axis=0, keepdims=True)
    p = jnp.exp(s - m)
    l = jnp.sum(p, axis=0, keepdims=True)
    acc = weighted_values(k0, p)

    def past_blocks(i, carry):
        m, l, acc = carry
        js = [i * MOBA_GROUP + g for g in range(MOBA_GROUP)]
        j0s = [pl.multiple_of(j * blk, blk) for j in js]
        s = jnp.concatenate([logits(j0, n - j) + sel_sc[pl.ds(j, 1), :] for j, j0 in zip(js, j0s)], axis=0)
        m_new = jnp.maximum(m, jnp.max(s, axis=0, keepdims=True))
        alpha = jnp.exp(m - m_new)
        p = jnp.exp(s - m_new)
        l = alpha * l + jnp.sum(p, axis=0, keepdims=True)
        acc = alpha * acc
        for g, j0 in enumerate(j0s):
            acc = acc + weighted_values(j0, p[g * blk:(g + 1) * blk])
        return m_new, l, acc

    ntrips = lax.shift_right_logical(n + MOBA_GROUP - 1, MOBA_GROUP.bit_length() - 1)
    m, l, acc = lax.fori_loop(0, ntrips, past_blocks, (m, l, acc))
    out = acc / l
    out_t = jnp.concatenate([out[:, h * blk:(h + 1) * blk] for h in range(nheads)], axis=0)
    o_ref[...] = out_t.T


def _moba(zb, bias):
    b, seq, _ = zb.shape
    blk = MOBA_BLOCK
    nb = seq // blk
    return pl.pallas_call(
        functools.partial(_moba_body, topb=min(MOBA_TOPK, nb)),
        grid=(b, nb),
        in_specs=[pl.BlockSpec((None, blk, GROUP_WIDTH), lambda i, j: (i, j, 0)),
                  pl.BlockSpec((None, seq, GROUP_WIDTH), lambda i, j: (i, 0, 1)),
                  pl.BlockSpec((None, seq, GROUP_WIDTH), lambda i, j: (i, 0, 2)),
                  _resident(bias.shape)],
        out_specs=pl.BlockSpec((None, blk, GROUP_WIDTH), lambda i, j: (i, j, 0)),
        out_shape=jax.ShapeDtypeStruct((b, seq, GROUP_WIDTH), F32),
        scratch_shapes=[pltpu.VMEM((N_HEADS_GROUP, seq, HEAD_DIM), BF16),
                        pltpu.VMEM((N_HEADS_GROUP, HEAD_DIM, seq), BF16),
                        pltpu.VMEM((N_HEADS_GROUP, nb, HEAD_DIM), F32),
                        pltpu.VMEM((nb, N_HEADS_GROUP * blk), F32)],
        compiler_params=_params(("parallel", "arbitrary")),
        name="moba",
    )(zb, zb, zb, bias)


HGRN_STEP = 256
_HGRN_LEVELS = (32, 16, 8, 4, 2, 1)


def _hgrn_constants():
    c = HGRN_CHUNK
    t = np.arange(c)[:, None]
    u = np.arange(c)[None, :]
    sel = [(u <= t),
           (u > t),
           np.ones((8, c), bool)]
    masks = []
    for m in _HGRN_LEVELS:
        r = (t // (2 * m)) * 2 * m + m - 1
        right = t > r
        sel.append(right & (u > r) & (u <= t))
        sel.append((~right) & (u > t) & (u <= r))
        s = u
        same_blk = (t // (2 * m)) == (s // (2 * m))
        masks.append(same_blk & right & (s <= r))
    masks.append(t == u)
    sel = np.concatenate(sel, axis=0).astype(np.float32)
    masks = np.stack([np.tile(mk, (HGRN_STEP // c, N_HEADS_GROUP)) for mk in masks]).astype(np.float32)
    hid = np.arange(GROUP_WIDTH) // HEAD_DIM
    blockdiag = (hid[:, None] == hid[None, :]).astype(np.float32)
    return sel, masks, blockdiag


def _hgrn_body(z_ref, lbl_ref, gain_ref, sel_ref, mask_ref, bd_ref, o_ref, st_sc, *, layer):
    c = HGRN_CHUNK
    w = GROUP_WIDTH
    nheads = N_HEADS_GROUP

    @pl.when(pl.program_id(1) == 0)
    def _():
        st_sc[...] = jnp.zeros(st_sc.shape, F32)

    logits = lbl_ref[...]
    e = jnp.exp(logits - jnp.max(logits, axis=0, keepdims=True))
    lb_w = e / jnp.sum(e, axis=0, keepdims=True)
    lb = jnp.sum(lb_w[0:layer + 1], axis=0, keepdims=True) - lb_w[0:1]

    bd = bd_ref[...] > 0.5
    bd_bf16 = bd_ref[...].astype(BF16)
    sel = sel_ref[...].astype(BF16)

    def blockdiag(x_bf16):
        return jnp.where(bd, jnp.concatenate([x_bf16] * nheads, axis=0), jnp.zeros((), BF16))

    chunks = [slice(r0, r0 + c) for r0 in range(0, z_ref.shape[0], c)]

    def per_chunk(fn):
        return jnp.concatenate([fn(rc) for rc in chunks], axis=0)

    q = z_ref[:, 0:w]
    f = lb + (1.0 - lb) * jax.nn.sigmoid(z_ref[:, w:2 * w])
    kin = 1.0 - f
    hi, mid, lo = _split3(jnp.log(f))
    decays = [jnp.exp(_dot(sel, hi[rc]) + _dot(sel, mid[rc]) + _dot(sel, lo[rc])) for rc in chunks]

    def decay_rows(base):
        return jnp.concatenate([d[base:base + c] for d in decays], axis=0)

    a = jnp.zeros(q.shape, F32)
    for li in range(len(_HGRN_LEVELS)):
        base = 2 * c + 8 + 2 * c * li
        qm = (q * decay_rows(base)).astype(BF16)
        km = (kin * decay_rows(base + c)).astype(BF16)
        a = a + mask_ref[li] * per_chunk(lambda rc: _dot_nt(qm[rc], blockdiag(km[rc])))
    q_bf16, kin_bf16 = q.astype(BF16), kin.astype(BF16)
    a = a + mask_ref[len(_HGRN_LEVELS)] * per_chunk(lambda rc: _dot_nt(q_bf16[rc], blockdiag(kin_bf16[rc])))

    a_bf16 = a.astype(BF16)
    v_bf16 = z_ref[:, 2 * w:3 * w].astype(BF16)
    o = per_chunk(lambda rc: _dot(a_bf16[rc], blockdiag(v_bf16[rc])))
    khat = (kin * decay_rows(c)).astype(BF16)
    deltas = [jnp.where(bd, _dot_tn(v_bf16[rc], khat[rc]), 0.0) for rc in chunks]
    q_dec = (q * decay_rows(0)).astype(BF16)
    st = st_sc[...]
    carried = []
    for ci, rc in enumerate(chunks):
        carried.append(_dot_nt(q_dec[rc], st.astype(BF16)))
        st = st * decays[ci][2 * c:2 * c + 1] + deltas[ci]
    st_sc[...] = st
    o = o + jnp.concatenate(carried, axis=0)

    oo = o * o
    oo_hi = oo.astype(BF16)
    oo_lo = (oo - oo_hi.astype(F32)).astype(BF16)
    ms = (_dot(oo_hi, bd_bf16) + _dot(oo_lo, bd_bf16)) * (1.0 / HEAD_DIM)
    g = z_ref[:, 3 * w:4 * w]
    o_ref[...] = o * lax.rsqrt(ms + EPS) * gain_ref[...] * (g * jax.nn.sigmoid(g))


def _hgrn(zc, lb_logits, gain, layer):
    b, seq, _ = zc.shape
    sel, masks, blockdiag = _hgrn_constants()
    return pl.pallas_call(
        functools.partial(_hgrn_body, layer=layer),
        grid=(b, seq // HGRN_STEP),
        in_specs=[pl.BlockSpec((None, HGRN_STEP, GC_W), lambda i, j: (i, j, 0)),
                  _resident(lb_logits.shape), _resident(gain.shape),
                  _resident(sel.shape), _resident(masks.shape), _resident(blockdiag.shape)],
        out_specs=pl.BlockSpec((None, HGRN_STEP, GROUP_WIDTH), lambda i, j: (i, j, 0)),
        out_shape=jax.ShapeDtypeStruct((b, seq, GROUP_WIDTH), F32),
        scratch_shapes=[pltpu.VMEM((GROUP_WIDTH, GROUP_WIDTH), F32)],
        compiler_params=_params(("parallel", "arbitrary")),
        name="hgrn",
    )(zc, lb_logits, gain, jnp.asarray(sel), jnp.asarray(masks), jnp.asarray(blockdiag))


DIL_GROUP = 4


def _dilated_body(q_ref, k_ref, v_ref, bias_ref, o_ref, m_sc, l_sc, acc_sc):
    span = DIL_SPAN
    seq = q_ref.shape[0]
    lane = lax.broadcasted_iota(I32, (1, LANES), 1)
    head_lanes = [lane < HEAD_DIM, lane >= HEAD_DIM]
    a_i = lax.broadcasted_iota(I32, (span, 2 * span), 0)
    c_i = lax.broadcasted_iota(I32, (span, 2 * span), 1)
    delta = a_i + span - c_i
    window = (delta >= 0) & (delta <= span)
    half_ones = [jnp.where(lane < HEAD_DIM, 1.0, 0.0).astype(BF16) * jnp.ones((2 * span, 1), BF16),
                 jnp.where(lane >= HEAD_DIM, 1.0, 0.0).astype(BF16) * jnp.ones((2 * span, 1), BF16)]
    last = len(DILATED_BRANCHES) - 1

    for bi, (_, r) in enumerate(DILATED_BRANCHES):
        shift = r.bit_length() - 1

        def rows(start, r=r):
            return pl.ds(start, span) if r == 1 else pl.ds(start, span, stride=r)

        def blocks(i, carry, bi=bi, r=r, shift=shift, rows=rows):
            grp = range(DIL_GROUP)
            starts, q, kcat, vcat, mask = [], [], [], [], []
            for g in grp:
                blk = i * DIL_GROUP + g
                j = blk & (r - 1)
                n = lax.shift_right_logical(blk, shift)
                start = j + r * span * n
                prev = jnp.maximum(start - r * span, j)
                starts.append(start)
                q.append(q_ref[rows(start), :] * ATTN_SCALE)
                kcat.append(jnp.concatenate([k_ref[rows(prev), :], k_ref[rows(start), :]], axis=0).astype(BF16))
                vcat.append(jnp.concatenate([v_ref[rows(prev), :], v_ref[rows(start), :]], axis=0))
                mask.append(window & ((c_i >= span) | (n > 0)))
            mask = jnp.concatenate([jnp.where(mk, 0.0, NEG) for mk in mask], axis=0)
            num = [jnp.zeros((span, LANES), F32) for _ in grp]
            den = [jnp.zeros((span, LANES), F32) for _ in grp]
            m_b = jnp.zeros((DIL_GROUP * span, LANES), F32)
            for h in range(2):
                bias = bias_ref[bi, :, h * 2 * span:(h + 1) * 2 * span]
                logits = jnp.concatenate(
                    [_dot_nt(jnp.where(head_lanes[h], q[g], 0.0).astype(BF16), kcat[g]) + bias for g in grp], axis=0)
                logits = logits + mask
                m = jnp.max(logits, axis=1, keepdims=True)
                p = jnp.exp(logits - m).astype(BF16)
                m_b = jnp.where(head_lanes[h], m, m_b)
                for g in grp:
                    pg = p[g * span:(g + 1) * span]
                    num[g] = num[g] + _dot(pg, jnp.where(head_lanes[h], vcat[g], 0.0).astype(BF16))
                    den[g] = den[g] + _dot(pg, half_ones[h])
            num = jnp.concatenate(num, axis=0)
            den = jnp.concatenate(den, axis=0)
            if bi == 0:
                m_new, l_new, acc_new = m_b, den, num
            else:
                m_old = jnp.concatenate([m_sc[rows(st), :] for st in starts], axis=0)
                l_old = jnp.concatenate([l_sc[rows(st), :] for st in starts], axis=0)
                acc_old = jnp.concatenate([acc_sc[rows(st), :] for st in starts], axis=0)
                m_new = jnp.maximum(m_old, m_b)
                w_old = jnp.exp(m_old - m_new)
                w_b = jnp.exp(m_b - m_new)
                l_new = w_old * l_old + w_b * den
                acc_new = w_old * acc_old + w_b * num
            out = acc_new / l_new if bi == last else None
            for g, st in enumerate(starts):
                sl = slice(g * span, (g + 1) * span)
                if bi == last:
                    o_ref[rows(st), :] = out[sl]
                else:
                    m_sc[rows(st), :] = m_new[sl]
                    l_sc[rows(st), :] = l_new[sl]
                    acc_sc[rows(st), :] = acc_new[sl]
            return carry

        lax.fori_loop(0, seq // (span * DIL_GROUP), blocks, 0)


def _dilated(zd, bias):
    b, seq, _ = zd.shape
    pairs = GROUP_WIDTH // LANES
    assert seq % max(w for w, _ in DILATED_BRANCHES) == 0

    def spec(which):
        return pl.BlockSpec((None, seq, LANES), lambda i, j: (i, 0, pairs * which + j))

    nbr, span, bias_w = bias.shape
    return pl.pallas_call(
        _dilated_body,
        grid=(b, pairs),
        in_specs=[spec(0), spec(1), spec(2),
                  pl.BlockSpec((nbr, span, bias_w // pairs), lambda i, j: (0, 0, j))],
        out_specs=pl.BlockSpec((None, seq, LANES), lambda i, j: (i, 0, j)),
        out_shape=jax.ShapeDtypeStruct((b, seq, GROUP_WIDTH), F32),
        scratch_shapes=[pltpu.VMEM((seq, LANES), F32)] * 3,
        compiler_params=_params(("parallel", "parallel")),
        name="dilated",
    )(zd, zd, zd, bias)


def _out_ffn_body(*refs, final):
    x_ref, oa_ref, ob_ref, oc_ref, od_ref, wo_ref, g_ref, wg_ref, wu_ref, wd_ref = refs[0:10]
    gf_ref = refs[10] if final else None
    o_ref = refs[-1]
    cat = jnp.concatenate([oa_ref[...], ob_ref[...], oc_ref[...], od_ref[...]], axis=1).astype(BF16)
    x = x_ref[...] + _dot(cat, wo_ref[...])
    x = _swiglu_residual(x, g_ref[...], wg_ref, wu_ref, wd_ref)
    if final:
        x = _rms(x, gf_ref[...])
    o_ref[...] = x


def _out_ffn(x2d, mixers, wo, gain, wg, wu, wd, gain_final, tm):
    m, d = x2d.shape
    tile = lambda w: pl.BlockSpec((tm, w), lambda i: (i, 0))
    final = gain_final is not None
    args = [x2d, *mixers, wo, gain, wg, wu, wd] + ([gain_final] if final else [])
    in_specs = ([tile(d)] + [tile(GROUP_WIDTH)] * 4
                + [_resident(wo.shape), _resident((1, d)), _resident(wg.shape), _resident(wu.shape),
                   _resident(wd.shape)] + ([_resident((1, d))] if final else []))
    return pl.pallas_call(
        functools.partial(_out_ffn_body, final=final),
        grid=(m // tm,),
        in_specs=in_specs,
        out_specs=tile(d),
        out_shape=jax.ShapeDtypeStruct((m, d), F32),
        compiler_params=_params(("parallel",)),
        name="out_ffn",
    )(*args)


def kernel(x, norm_ffn1, ffn1_gate, ffn1_up, ffn1_down, norm_mix, w_in, ckv_norm, w_kv_up, hgrn_lb_logits,
           hgrn_norm, w_out, norm_ffn2, ffn2_gate, ffn2_up, ffn2_down, rel_bias, norm_final):
    b, seq, d = x.shape
    depth = w_in.shape[0]
    nh = N_HEADS_GROUP
    m = b * seq
    tm = min(512, seq)
    topk = min(TOPK_MAX, seq // 4)

    toe_ids = _toeplitz_bucket_ids(seq)
    bias_a = _bias_tiles(rel_bias, toe_ids, 0)
    bias_b = _bias_tiles(rel_bias, toe_ids, nh)
    bias_d = _bias_tiles(rel_bias, _dilated_bucket_ids(), 2 * nh)

    x2d = x.reshape(m, d)
    for l in range(depth):
        x2d = _ffn(x2d, norm_ffn1[l][None], ffn1_gate[l].astype(BF16), ffn1_up[l].astype(BF16),
                   ffn1_down[l].astype(BF16), tm)
        za, zb, zc, zd = _inproj(x2d, norm_mix[l][None], _pack_w_in(w_in[l]), tm)
        za, zb, zc, zd = (z.reshape(b, seq, -1) for z in (za, zb, zc, zd))

        w_up = w_kv_up[l].reshape(KV_RANK, 2, nh, HEAD_DIM)
        wuk_t = jnp.transpose(w_up[:, 0], (1, 2, 0)).astype(BF16)
        wuv_t = jnp.transpose(w_up[:, 1], (1, 2, 0)).astype(BF16)
        mixers = [_dsa(za, ckv_norm[l][None], wuk_t, wuv_t, bias_a, topk), _moba(zb, bias_b),
                  _hgrn(zc, hgrn_lb_logits, hgrn_norm[l][None], l), _dilated(zd, bias_d)]
        mixers = [o.reshape(m, GROUP_WIDTH) for o in mixers]

        gain_final = norm_final[None] if l == depth - 1 else None
        x2d = _out_ffn(x2d, mixers, w_out[l].astype(BF16), norm_ffn2[l][None],
                       ffn2_gate[l].astype(BF16), ffn2_up[l].astype(BF16), ffn2_down[l].astype(BF16),
                       gain_final, tm)
    return x2d.reshape(b, seq, d)
```

```python
import functools
import math

import jax
import jax.numpy as jnp
import numpy as np
from jax import lax
from jax.experimental import pallas as pl
from jax.experimental.pallas import tpu as pltpu

F32 = jnp.float32
BF16 = jnp.bfloat16
I32 = jnp.int32

HEAD_DIM = 64
N_GROUPS = 4
GROUP_WIDTH = 256
N_HEADS_GROUP = GROUP_WIDTH // HEAD_DIM
ATTN_SCALE = HEAD_DIM ** -0.5
KV_RANK = 128
IDX_HEADS = 8
IDX_DIM = 32
TOPK_MAX = 256
MOBA_BLOCK = 256
MOBA_TOPK = 3
HGRN_CHUNK = 64
DILATED_BRANCHES = ((128, 1), (512, 4), (2048, 16))
DIL_SPAN = 128
N_BUCKETS = 32
MAX_DISTANCE = 2048
EPS = 1e-6

LANES = 128
VMEM_LIMIT_BYTES = 56 * 1024 * 1024

NEG = -1e30
INT_MIN = -2 ** 31

GA_W, GB_W, GC_W, GD_W = 768, 768, 1024, 768
A_IW_LANE = 32


def _dot(a, b):
    return jnp.dot(a, b, preferred_element_type=F32)


def _dot_nt(a, b):
    return lax.dot_general(a, b, (((1,), (1,)), ((), ())), preferred_element_type=F32)


def _dot_tn(a, b):
    return lax.dot_general(a, b, (((0,), (0,)), ((), ())), preferred_element_type=F32)


def _rms(x, gain):
    return x * lax.rsqrt(jnp.mean(x * x, axis=-1, keepdims=True) + EPS) * gain


def _split3(x):
    hi = x.astype(BF16)
    r1 = x - hi.astype(F32)
    mid = r1.astype(BF16)
    lo = (r1 - mid.astype(F32)).astype(BF16)
    return hi, mid, lo


def _params(sem):
    return pltpu.CompilerParams(dimension_semantics=sem, vmem_limit_bytes=VMEM_LIMIT_BYTES)


def _resident(shape):
    nd = len(shape)
    return pl.BlockSpec(shape, lambda *_: (0,) * nd, pipeline_mode=pl.Buffered(1))


def _bucket_np(dist):
    max_exact = N_BUCKETS // 2
    n = np.maximum(dist, 0)
    nf = np.maximum(n, 1).astype(np.float64)
    large = max_exact + (np.log(nf / max_exact) / math.log(MAX_DISTANCE / max_exact)
                         * (N_BUCKETS - max_exact)).astype(np.int64)
    large = np.minimum(large, N_BUCKETS - 1)
    return np.where(n < max_exact, n, large).astype(np.int32)


def _toeplitz_bucket_ids(seq):
    nj = seq // LANES
    s = np.arange(LANES)[:, None]
    q = np.arange(LANES)[None, :]
    j = np.arange(nj)[:, None, None]
    return _bucket_np(LANES * j + q - s)


def _dilated_bucket_ids():
    a = np.arange(DIL_SPAN)[:, None]
    c = np.arange(2 * DIL_SPAN)[None, :]
    delta = a + DIL_SPAN - c
    return np.stack([_bucket_np(delta * r) for _, r in DILATED_BRANCHES])


def _bias_tiles_body(tab_ref, ids_ref, o_ref, *, head0):
    ids = ids_ref[0]
    c = ids.shape[1]
    for h in range(N_HEADS_GROUP):
        acc = jnp.zeros(ids.shape, F32)
        for k in range(N_BUCKETS):
            acc = jnp.where(ids == k, tab_ref[k, head0 + h], acc)
        o_ref[0, :, h * c:(h + 1) * c] = acc


def _bias_tiles(rel_bias, ids, head0):
    n, r, c = ids.shape
    return pl.pallas_call(
        functools.partial(_bias_tiles_body, head0=head0),
        grid=(n,),
        in_specs=[pl.BlockSpec(memory_space=pltpu.SMEM),
                  pl.BlockSpec((1, r, c), lambda i: (i, 0, 0))],
        out_specs=pl.BlockSpec((1, r, N_HEADS_GROUP * c), lambda i: (i, 0, 0)),
        out_shape=jax.ShapeDtypeStruct((n, r, N_HEADS_GROUP * c), F32),
        compiler_params=_params(("arbitrary",)),
        name="bias_tiles",
    )(rel_bias, jnp.asarray(ids))


FFN_CHUNK = 256


def _swiglu_residual(x, gain, wg_ref, wu_ref, wd_ref):
    h = _rms(x, gain).astype(BF16)
    d_ff = wg_ref.shape[1]
    acc = jnp.zeros(x.shape, F32)
    for c0 in range(0, d_ff, FFN_CHUNK):
        g = _dot(h, wg_ref[:, c0:c0 + FFN_CHUNK])
        u = _dot(h, wu_ref[:, c0:c0 + FFN_CHUNK])
        a = (g * jax.nn.sigmoid(g) * u).astype(BF16)
        acc = acc + _dot(a, wd_ref[c0:c0 + FFN_CHUNK, :])
    return x + 0.5 * acc


def _ffn_body(x_ref, g_ref, wg_ref, wu_ref, wd_ref, o_ref):
    o_ref[...] = _swiglu_residual(x_ref[...], g_ref[...], wg_ref, wu_ref, wd_ref)


def _ffn(x2d, gain, wg, wu, wd, tm):
    m, d = x2d.shape
    return pl.pallas_call(
        _ffn_body,
        grid=(m // tm,),
        in_specs=[pl.BlockSpec((tm, d), lambda i: (i, 0)),
                  _resident((1, d)), _resident(wg.shape), _resident(wu.shape), _resident(wd.shape)],
        out_specs=pl.BlockSpec((tm, d), lambda i: (i, 0)),
        out_shape=jax.ShapeDtypeStruct((m, d), F32),
        compiler_params=_params(("parallel",)),
        name="ffn",
    )(x2d, gain, wg, wu, wd)


def _inproj_body(x_ref, g_ref, w_ref, za_ref, zb_ref, zc_ref, zd_ref):
    h = _rms(x_ref[...], g_ref[...]).astype(BF16)
    c0 = 0
    for ref in (za_ref, zb_ref, zc_ref, zd_ref):
        w = ref.shape[1]
        ref[...] = _dot(h, w_ref[:, c0:c0 + w])
        c0 += w


def _inproj(x2d, gain, w_packed, tm):
    m, d = x2d.shape
    widths = (GA_W, GB_W, GC_W, GD_W)
    return pl.pallas_call(
        _inproj_body,
        grid=(m // tm,),
        in_specs=[pl.BlockSpec((tm, d), lambda i: (i, 0)), _resident((1, d)), _resident(w_packed.shape)],
        out_specs=[pl.BlockSpec((tm, w), lambda i: (i, 0)) for w in widths],
        out_shape=[jax.ShapeDtypeStruct((m, w), F32) for w in widths],
        compiler_params=_params(("parallel",)),
        name="inproj",
    )(x2d, gain, w_packed)


def _pack_w_in(w):
    d = w.shape[0]
    pad = jnp.zeros((d, GA_W - 680), w.dtype)
    return jnp.concatenate([w[:, 0:256], w[:, 384:640], w[:, 256:384], w[:, 640:680], pad, w[:, 680:]],
                           axis=1).astype(BF16)


def _online_softmax_step(s, vt_bf16, m, l, acc):
    m_new = jnp.maximum(m, jnp.max(s, axis=0, keepdims=True))
    alpha = jnp.exp(m - m_new)
    p = jnp.exp(s - m_new)
    l = alpha * l + jnp.sum(p, axis=0, keepdims=True)
    acc = alpha * acc + _dot(vt_bf16, p.astype(BF16))
    return m_new, l, acc


DSA_TQ = 256
DSA_TK = LANES
DSA_TILE_PAD = 4
DSA_SCORE_GROUP = 4
DSA_COUNT_GROUP = 4
DSA_ATTN_GROUP = 4


def _dsa_body(q_ref, iq_ref, iwq_ref, ckv_ref, ikw_ref, gain_ref, wuk_ref, wuvt_ref, bias_ref,
              o_ref, c_sc, ct_sc, ik_sc, key_sc, *, topk):
    tq, tk = DSA_TQ, DSA_TK
    n = pl.program_id(1)
    seq = ckv_ref.shape[0]
    nheads = N_HEADS_GROUP
    lane = lax.broadcasted_iota(I32, (1, LANES), 1)

    @pl.when(n == 0)
    def _():
        for r0 in range(0, seq, tk):
            c = _rms(ckv_ref[r0:r0 + tk, :], gain_ref[...])
            c_sc[r0:r0 + tk, :] = c.astype(BF16)
            ct_sc[:, r0:r0 + tk] = c.T.astype(BF16)
            ik = jnp.where(lane < IDX_DIM, ikw_ref[r0:r0 + tk, :], 0.0)
            ik4 = ik
            for rep in range(1, LANES // IDX_DIM):
                ik4 = ik4 + pltpu.roll(ik, rep * IDX_DIM, 1)
            ik_sc[r0:r0 + tk, :] = ik4.astype(BF16)

    iq = iq_ref[...]
    iw_t = (iwq_ref[...] * (IDX_HEADS * IDX_DIM) ** -0.5).T
    tiles_per_q = tq // tk
    heads_per_group = LANES // IDX_DIM
    lhs = []
    for h in range(IDX_HEADS):
        blk = iq[:, LANES * (h // heads_per_group):LANES * (h // heads_per_group + 1)]
        lo = IDX_DIM * (h % heads_per_group)
        lhs.append(jnp.where((lane >= lo) & (lane < lo + IDX_DIM), blk, 0.0).astype(BF16))
    lhs = jnp.concatenate(lhs, axis=0)
    qpos = n * tq + lax.broadcasted_iota(I32, (1, tq), 1)
    srow = lax.broadcasted_iota(I32, (tk, 1), 0)
    pad_shift = DSA_TILE_PAD.bit_length() - 1
    ntiles = lax.shift_left(
        lax.shift_right_logical((n + 1) * tiles_per_q + DSA_TILE_PAD - 1, pad_shift), pad_shift)

    def over_key_tiles(group, body, carry):
        ngroups = lax.shift_right_logical(ntiles, group.bit_length() - 1)
        return lax.fori_loop(0, ngroups, lambda i, c: body(i * group, group, c), carry)

    def score_tiles(kt, g, carry):
        k0 = pl.multiple_of(kt * tk, tk)
        s = _dot_nt(ik_sc[pl.ds(k0, g * tk), :], lhs)
        idx = jnp.zeros((g * tk, tq), F32)
        for h in range(IDX_HEADS):
            idx = idx + iw_t[A_IW_LANE + h:A_IW_LANE + h + 1, :] * jnp.maximum(s[:, h * tq:(h + 1) * tq], 0.0)
        bits = pltpu.bitcast(idx, I32)
        keys = bits ^ ((bits >> 31) & 0x7FFFFFFF)
        spos = k0 + lax.broadcasted_iota(I32, (g * tk, 1), 0)
        key_sc[pl.ds(k0, g * tk), :] = jnp.where(spos <= qpos, keys, INT_MIN)
        return carry

    over_key_tiles(DSA_SCORE_GROUP, score_tiles, 0)

    def count(pred):
        def body(kt, g, acc):
            for i in range(g):
                k0 = pl.multiple_of((kt + i) * tk, tk)
                acc = acc + jnp.where(pred(key_sc[pl.ds(k0, tk), :], k0), 1, 0)
            return acc
        acc = over_key_tiles(DSA_COUNT_GROUP, body, jnp.zeros((tk, tq), I32))
        return jnp.sum(acc, axis=0, keepdims=True)

    def bit_iter(i, t):
        cand = t | lax.shift_left(jnp.int32(1), 31 - i)
        cand_s = cand ^ INT_MIN
        cnt = count(lambda keys, k0: keys >= cand_s)
        return jnp.where(cnt >= topk, cand, t)

    t = lax.fori_loop(0, 32, bit_iter, jnp.zeros((1, tq), I32))
    tau = t ^ INT_MIN
    cnt_gt = count(lambda keys, k0: keys > tau)
    cnt_eq = count(lambda keys, k0: keys == tau)
    need = topk - cnt_gt
    tie_q = (cnt_eq > need) & (t != 0)

    @pl.when(jnp.max(tie_q.astype(I32)) > 0)
    def _():
        def pos_iter(i, p):
            cand = p | lax.shift_left(jnp.int32(1), (seq.bit_length() - 1) - i)
            cnt = count(lambda keys, k0: (keys == tau) & (k0 + srow < cand))
            return jnp.where(cnt < need, cand, p)
        p = lax.fori_loop(0, seq.bit_length(), pos_iter, jnp.zeros((1, tq), I32))
        p = jnp.where(tie_q, p, seq)

        def drop(kt, carry):
            k0 = pl.multiple_of(kt * tk, tk)
            keys = key_sc[pl.ds(k0, tk), :]
            key_sc[pl.ds(k0, tk), :] = jnp.where((keys == tau) & (k0 + srow > p), INT_MIN, keys)
            return carry
        lax.fori_loop(0, ntiles, drop, 0)

    tau_sel = jnp.maximum(tau, INT_MIN + 1)

    q = q_ref[...]
    qs = []
    for h in range(nheads):
        qh = q[:, h * HEAD_DIM:(h + 1) * HEAD_DIM].astype(BF16)
        qs.append((_dot(qh, wuk_ref[h]) * ATTN_SCALE).astype(BF16))
    qs = jnp.concatenate(qs, axis=0)
    cols = nheads * tq

    def bias_rows(kt):
        parts = [bias_ref[jnp.maximum(n * tiles_per_q + qa - kt, 0)] for qa in range(tiles_per_q)]
        return jnp.concatenate([p[:, h * LANES:(h + 1) * LANES] for h in range(nheads) for p in parts], axis=1)

    def attn_tiles(kt, g, carry):
        m, l, acc = carry
        k0 = pl.multiple_of(kt * tk, tk)
        s = _dot_nt(c_sc[pl.ds(k0, g * tk), :], qs)
        drop = jnp.where(key_sc[pl.ds(k0, g * tk), :] >= tau_sel, 0.0, NEG)
        bias = jnp.concatenate([bias_rows(kt + i) for i in range(g)], axis=0)
        s = s + bias + jnp.concatenate([drop] * nheads, axis=1)
        return _online_softmax_step(s, ct_sc[:, pl.ds(k0, g * tk)], m, l, acc)

    m0 = jnp.full((1, cols), NEG, F32)
    l0 = jnp.zeros((1, cols), F32)
    acc0 = jnp.zeros((KV_RANK, cols), F32)
    m, l, acc = over_key_tiles(DSA_ATTN_GROUP, attn_tiles, (m0, l0, acc0))
    ctx = (acc / l).astype(BF16)
    out_t = jnp.concatenate([_dot(wuvt_ref[h], ctx[:, h * tq:(h + 1) * tq]) for h in range(nheads)], axis=0)
    o_ref[...] = out_t.T


def _dsa(za, ckv_gain, wuk_t, wuv_t, bias, topk):
    b, seq, _ = za.shape
    tq = DSA_TQ
    assert seq % tq == 0 and seq % (DSA_TK * DSA_TILE_PAD) == 0
    return pl.pallas_call(
        functools.partial(_dsa_body, topk=topk),
        grid=(b, seq // tq),
        in_specs=[pl.BlockSpec((None, tq, 256), lambda i, j: (i, j, 0)),
                  pl.BlockSpec((None, tq, 256), lambda i, j: (i, j, 1)),
                  pl.BlockSpec((None, tq, LANES), lambda i, j: (i, j, 5)),
                  pl.BlockSpec((None, seq, LANES), lambda i, j: (i, 0, 4)),
                  pl.BlockSpec((None, seq, LANES), lambda i, j: (i, 0, 5)),
                  _resident(ckv_gain.shape), _resident(wuk_t.shape), _resident(wuv_t.shape),
                  _resident(bias.shape)],
        out_specs=pl.BlockSpec((None, tq, GROUP_WIDTH), lambda i, j: (i, j, 0)),
        out_shape=jax.ShapeDtypeStruct((b, seq, GROUP_WIDTH), F32),
        scratch_shapes=[pltpu.VMEM((seq, KV_RANK), BF16), pltpu.VMEM((KV_RANK, seq), BF16),
                        pltpu.VMEM((seq, LANES), BF16), pltpu.VMEM((seq, tq), I32)],
        compiler_params=_params(("parallel", "arbitrary")),
        name="dsa",
    )(za, za, za, za, za, ckv_gain, wuk_t, wuv_t, bias)


MOBA_GROUP = 2


def _moba_bias_tile(bias_ref, d):
    lo = jnp.maximum(2 * d - 1, 0)
    t_same, t_next, t_prev = bias_ref[2 * d], bias_ref[2 * d + 1], bias_ref[lo]
    top, bot = [], []
    for h in range(N_HEADS_GROUP):
        hs = slice(h * LANES, (h + 1) * LANES)
        top += [t_same[:, hs], t_next[:, hs]]
        bot += [t_prev[:, hs], t_same[:, hs]]
    return jnp.concatenate([jnp.concatenate(top, axis=1), jnp.concatenate(bot, axis=1)], axis=0)


def _moba_body(q_ref, k_ref, v_ref, bias_ref, o_ref, kh_sc, vt_sc, km_sc, sel_sc, *, topb):
    blk = MOBA_BLOCK
    n = pl.program_id(1)
    seq = k_ref.shape[0]
    nb = seq // blk
    nheads = N_HEADS_GROUP

    @pl.when(n == 0)
    def _():
        for j in range(nb):
            kb = k_ref[j * blk:(j + 1) * blk, :]
            vbt = v_ref[j * blk:(j + 1) * blk, :].T
            km = jnp.mean(kb, axis=0, keepdims=True)
            for h in range(nheads):
                sl = slice(h * HEAD_DIM, (h + 1) * HEAD_DIM)
                kh_sc[h, j * blk:(j + 1) * blk, :] = kb[:, sl].astype(BF16)
                vt_sc[h, :, j * blk:(j + 1) * blk] = vbt[sl, :].astype(BF16)
                km_sc[h, j:j + 1, :] = km[:, sl]

    q = q_ref[...]
    rowid = lax.broadcasted_iota(I32, (nb, blk), 0)
    qs = []
    for h in range(nheads):
        qh = q[:, h * HEAD_DIM:(h + 1) * HEAD_DIM]
        km_hi, km_mid, _ = _split3(km_sc[h])
        q_hi, q_mid, _ = _split3(qh)
        gate = _dot_nt(km_hi, q_hi) + _dot_nt(km_mid, q_hi) + _dot_nt(km_hi, q_mid)
        gate = jnp.where(rowid < n, gate, -jnp.inf)
        rank = jnp.zeros((nb, blk), I32)
        for mth in range(nb):
            gm = gate[mth:mth + 1, :]
            beats = (gm > gate) | ((gm == gate) & (mth < rowid))
            rank = rank + jnp.where(beats, 1, 0)
        sel_sc[:, h * blk:(h + 1) * blk] = jnp.where((rank < topb) & (rowid < n), 0.0, NEG)
        qs.append((qh * ATTN_SCALE).astype(BF16))

    def logits(j0, d):
        s = jnp.concatenate([_dot_nt(kh_sc[h, pl.ds(j0, blk), :], qs[h]) for h in range(nheads)], axis=1)
        return s + _moba_bias_tile(bias_ref, d)

    def weighted_values(j0, p):
        p = p.astype(BF16)
        return jnp.concatenate([_dot(vt_sc[h, :, pl.ds(j0, blk)], p[:, h * blk:(h + 1) * blk])
                                for h in range(nheads)], axis=1)

    k0 = pl.multiple_of(n * blk, blk)
    wide = (blk, nheads * blk)
    causal = lax.broadcasted_iota(I32, wide, 0) <= (lax.broadcasted_iota(I32, wide, 1) & (blk - 1))
    s = jnp.where(causal, logits(k0, 0), NEG)
    m = jnp.max(s, axis=0, keepdims=True)
    p = jnp.exp(s - m)
    l = jnp.sum(p, axis=0, keepdims=True)
    acc = weighted_values(k0, p)

    def past_blocks(i, carry):
        m, l, acc = carry
        js = [i * MOBA_GROUP + g for g in range(MOBA_GROUP)]
        j0s = [pl.multiple_of(j * blk, blk) for j in js]
        s = jnp.concatenate([logits(j0, n - j) + sel_sc[pl.ds(j, 1), :] for j, j0 in zip(js, j0s)], axis=0)
        m_new = jnp.maximum(m, jnp.max(s, axis=0, keepdims=True))
        alpha = jnp.exp(m - m_new)
        p = jnp.exp(s - m_new)
        l = alpha * l + jnp.sum(p, axis=0, keepdims=True)
        acc = alpha * acc
        for g, j0 in enumerate(j0s):
            acc = acc + weighted_values(j0, p[g * blk:(g + 1) * blk])
        return m_new, l, acc

    ntrips = lax.shift_right_logical(n + MOBA_GROUP - 1, MOBA_GROUP.bit_length() - 1)
    m, l, acc = lax.fori_loop(0, ntrips, past_blocks, (m, l, acc))
    out = acc / l
    out_t = jnp.concatenate([out[:, h * blk:(h + 1) * blk] for h in range(nheads)], axis=0)
    o_ref[...] = out_t.T


def _moba(zb, bias):
    b, seq, _ = zb.shape
    blk = MOBA_BLOCK
    nb = seq // blk
    return pl.pallas_call(
        functools.partial(_moba_body, topb=min(MOBA_TOPK, nb)),
        grid=(b, nb),
        in_specs=[pl.BlockSpec((None, blk, GROUP_WIDTH), lambda i, j: (i, j, 0)),
                  pl.BlockSpec((None, seq, GROUP_WIDTH), lambda i, j: (i, 0, 1)),
                  pl.BlockSpec((None, seq, GROUP_WIDTH), lambda i, j: (i, 0, 2)),
                  _resident(bias.shape)],
        out_specs=pl.BlockSpec((None, blk, GROUP_WIDTH), lambda i, j: (i, j, 0)),
        out_shape=jax.ShapeDtypeStruct((b, seq, GROUP_WIDTH), F32),
        scratch_shapes=[pltpu.VMEM((N_HEADS_GROUP, seq, HEAD_DIM), BF16),
                        pltpu.VMEM((N_HEADS_GROUP, HEAD_DIM, seq), BF16),
                        pltpu.VMEM((N_HEADS_GROUP, nb, HEAD_DIM), F32),
                        pltpu.VMEM((nb, N_HEADS_GROUP * blk), F32)],
        compiler_params=_params(("parallel", "arbitrary")),
        name="moba",
    )(zb, zb, zb, bias)


HGRN_STEP = 256
_HGRN_SUBS = (16, 4, 1)
_HGRN_FANOUT = 4


def _hgrn_constants():
    c = HGRN_CHUNK
    fan = _HGRN_FANOUT
    t = np.arange(c)[:, None]
    u = np.arange(c)[None, :]
    sel = [(u <= t),
           (u > t),
           np.ones((8, c), bool)]
    masks = []
    for sub in _HGRN_SUBS:
        pos = (t // sub) % fan
        if sub > 1:
            e_own = (t // sub) * sub + sub - 1
            sel.append((u > t) & (u <= e_own))
        for j in range(fan - 1):
            e_j = (t // (fan * sub)) * fan * sub + j * sub + sub - 1
            later = pos > j
            sel.append(later & (u > e_j) & (u <= t))
            s = u
            same_blk = (t // (fan * sub)) == (s // (fan * sub))
            masks.append(same_blk & later & ((s // sub) % fan == j))
    masks.append(t == u)
    sel = np.concatenate(sel, axis=0).astype(np.float32)
    masks = np.stack([np.tile(mk, (HGRN_STEP // c, N_HEADS_GROUP)) for mk in masks]).astype(np.float32)
    hid = np.arange(GROUP_WIDTH) // HEAD_DIM
    blockdiag = (hid[:, None] == hid[None, :]).astype(np.float32)
    return sel, masks, blockdiag


def _hgrn_body(z_ref, lbl_ref, gain_ref, sel_ref, mask_ref, bd_ref, o_ref, st_sc, *, layer):
    c = HGRN_CHUNK
    w = GROUP_WIDTH
    nheads = N_HEADS_GROUP

    @pl.when(pl.program_id(1) == 0)
    def _():
        st_sc[...] = jnp.zeros(st_sc.shape, F32)

    logits = lbl_ref[...]
    e = jnp.exp(logits - jnp.max(logits, axis=0, keepdims=True))
    lb_w = e / jnp.sum(e, axis=0, keepdims=True)
    lb = jnp.sum(lb_w[0:layer + 1], axis=0, keepdims=True) - lb_w[0:1]

    bd = bd_ref[...] > 0.5
    bd_bf16 = bd_ref[...].astype(BF16)
    sel = sel_ref[...].astype(BF16)

    def blockdiag(x_bf16):
        return jnp.where(bd, jnp.concatenate([x_bf16] * nheads, axis=0), jnp.zeros((), BF16))

    chunks = [slice(r0, r0 + c) for r0 in range(0, z_ref.shape[0], c)]

    def per_chunk(fn):
        return jnp.concatenate([fn(rc) for rc in chunks], axis=0)

    q = z_ref[:, 0:w]
    f = lb + (1.0 - lb) * jax.nn.sigmoid(z_ref[:, w:2 * w])
    kin = 1.0 - f
    hi, mid, lo = _split3(jnp.log(f))
    sel3 = jnp.concatenate([sel] * 3, axis=1)
    decays = [jnp.exp(_dot(sel3, jnp.concatenate([hi[rc], mid[rc], lo[rc]], axis=0))) for rc in chunks]

    def decay_rows(base):
        return jnp.concatenate([d[base:base + c] for d in decays], axis=0)

    a = jnp.zeros(q.shape, F32)
    base = 2 * c + 8
    mi = 0
    for sub in _HGRN_SUBS:
        if sub > 1:
            km = (kin * decay_rows(base)).astype(BF16)
            base += c
        else:
            km = kin.astype(BF16)
        variants = [(q * decay_rows(base + j * c)).astype(BF16) for j in range(_HGRN_FANOUT - 1)]
        base += (_HGRN_FANOUT - 1) * c
        if sub == 1:
            variants.append(q.astype(BF16))
        prods = [_dot_nt(jnp.concatenate([v[rc] for v in variants], axis=0), blockdiag(km[rc])) for rc in chunks]
        for j in range(len(variants)):
            a = a + mask_ref[mi] * jnp.concatenate([p[j * c:(j + 1) * c] for p in prods], axis=0)
            mi += 1

    a_bf16 = a.astype(BF16)
    v_bf16 = z_ref[:, 2 * w:3 * w].astype(BF16)
    o = per_chunk(lambda rc: _dot(a_bf16[rc], blockdiag(v_bf16[rc])))
    khat = (kin * decay_rows(c)).astype(BF16)
    deltas = [jnp.where(bd, _dot_tn(v_bf16[rc], khat[rc]), 0.0) for rc in chunks]
    q_dec = (q * decay_rows(0)).astype(BF16)
    st = st_sc[...]
    carried = []
    for ci, rc in enumerate(chunks):
        carried.append(_dot_nt(q_dec[rc], st.astype(BF16)))
        st = st * decays[ci][2 * c:2 * c + 1] + deltas[ci]
    st_sc[...] = st
    o = o + jnp.concatenate(carried, axis=0)

    oo = o * o
    oo_hi = oo.astype(BF16)
    oo_lo = (oo - oo_hi.astype(F32)).astype(BF16)
    ms = (_dot(oo_hi, bd_bf16) + _dot(oo_lo, bd_bf16)) * (1.0 / HEAD_DIM)
    g = z_ref[:, 3 * w:4 * w]
    o_ref[...] = o * lax.rsqrt(ms + EPS) * gain_ref[...] * (g * jax.nn.sigmoid(g))


def _hgrn(zc, lb_logits, gain, layer):
    b, seq, _ = zc.shape
    sel, masks, blockdiag = _hgrn_constants()
    return pl.pallas_call(
        functools.partial(_hgrn_body, layer=layer),
        grid=(b, seq // HGRN_STEP),
        in_specs=[pl.BlockSpec((None, HGRN_STEP, GC_W), lambda i, j: (i, j, 0)),
                  _resident(lb_logits.shape), _resident(gain.shape),
                  _resident(sel.shape), _resident(masks.shape), _resident(blockdiag.shape)],
        out_specs=pl.BlockSpec((None, HGRN_STEP, GROUP_WIDTH), lambda i, j: (i, j, 0)),
        out_shape=jax.ShapeDtypeStruct((b, seq, GROUP_WIDTH), F32),
        scratch_shapes=[pltpu.VMEM((GROUP_WIDTH, GROUP_WIDTH), F32)],
        compiler_params=_params(("parallel", "arbitrary")),
        name="hgrn",
    )(zc, lb_logits, gain, jnp.asarray(sel), jnp.asarray(masks), jnp.asarray(blockdiag))


DIL_GROUP = 4


def _dilated_body(q_ref, k_ref, v_ref, bias_ref, o_ref, m_sc, l_sc, acc_sc):
    span = DIL_SPAN
    seq = q_ref.shape[0]
    lane = lax.broadcasted_iota(I32, (1, LANES), 1)
    head_lanes = [lane < HEAD_DIM, lane >= HEAD_DIM]
    a_i = lax.broadcasted_iota(I32, (span, 2 * span), 0)
    c_i = lax.broadcasted_iota(I32, (span, 2 * span), 1)
    delta = a_i + span - c_i
    window = (delta >= 0) & (delta <= span)
    half_ones = [jnp.where(lane < HEAD_DIM, 1.0, 0.0).astype(BF16) * jnp.ones((2 * span, 1), BF16),
                 jnp.where(lane >= HEAD_DIM, 1.0, 0.0).astype(BF16) * jnp.ones((2 * span, 1), BF16)]
    last = len(DILATED_BRANCHES) - 1

    for bi, (_, r) in enumerate(DILATED_BRANCHES):
        shift = r.bit_length() - 1

        def rows(start, r=r):
            return pl.ds(start, span) if r == 1 else pl.ds(start, span, stride=r)

        def blocks(i, carry, bi=bi, r=r, shift=shift, rows=rows):
            grp = range(DIL_GROUP)
            starts, q, kcat, vcat, mask = [], [], [], [], []
            for g in grp:
                blk = i * DIL_GROUP + g
                j = blk & (r - 1)
                n = lax.shift_right_logical(blk, shift)
                start = j + r * span * n
                prev = jnp.maximum(start - r * span, j)
                starts.append(start)
                q.append(q_ref[rows(start), :] * ATTN_SCALE)
                kcat.append(jnp.concatenate([k_ref[rows(prev), :], k_ref[rows(start), :]], axis=0).astype(BF16))
                vcat.append(jnp.concatenate([v_ref[rows(prev), :], v_ref[rows(start), :]], axis=0))
                mask.append(window & ((c_i >= span) | (n > 0)))
            mask = jnp.concatenate([jnp.where(mk, 0.0, NEG) for mk in mask], axis=0)
            num = [jnp.zeros((span, LANES), F32) for _ in grp]
            den = [jnp.zeros((span, LANES), F32) for _ in grp]
            m_b = jnp.zeros((DIL_GROUP * span, LANES), F32)
            for h in range(2):
                bias = bias_ref[bi, :, h * 2 * span:(h + 1) * 2 * span]
                logits = jnp.concatenate(
                    [_dot_nt(jnp.where(head_lanes[h], q[g], 0.0).astype(BF16), kcat[g]) + bias for g in grp], axis=0)
                logits = logits + mask
                m = jnp.max(logits, axis=1, keepdims=True)
                p = jnp.exp(logits - m).astype(BF16)
                m_b = jnp.where(head_lanes[h], m, m_b)
                for g in grp:
                    pg = p[g * span:(g + 1) * span]
                    num[g] = num[g] + _dot(pg, jnp.where(head_lanes[h], vcat[g], 0.0).astype(BF16))
                    den[g] = den[g] + _dot(pg, half_ones[h])
            num = jnp.concatenate(num, axis=0)
            den = jnp.concatenate(den, axis=0)
            if bi == 0:
                m_new, l_new, acc_new = m_b, den, num
            else:
                m_old = jnp.concatenate([m_sc[rows(st), :] for st in starts], axis=0)
                l_old = jnp.concatenate([l_sc[rows(st), :] for st in starts], axis=0)
                acc_old = jnp.concatenate([acc_sc[rows(st), :] for st in starts], axis=0)
                m_new = jnp.maximum(m_old, m_b)
                w_old = jnp.exp(m_old - m_new)
                w_b = jnp.exp(m_b - m_new)
                l_new = w_old * l_old + w_b * den
                acc_new = w_old * acc_old + w_b * num
            out = acc_new / l_new if bi == last else None
            for g, st in enumerate(starts):
                sl = slice(g * span, (g + 1) * span)
                if bi == last:
                    o_ref[rows(st), :] = out[sl]
                else:
                    m_sc[rows(st), :] = m_new[sl]
                    l_sc[rows(st), :] = l_new[sl]
                    acc_sc[rows(st), :] = acc_new[sl]
            return carry

        lax.fori_loop(0, seq // (span * DIL_GROUP), blocks, 0)


def _dilated(zd, bias):
    b, seq, _ = zd.shape
    pairs = GROUP_WIDTH // LANES
    assert seq % max(w for w, _ in DILATED_BRANCHES) == 0

    def spec(which):
        return pl.BlockSpec((None, seq, LANES), lambda i, j: (i, 0, pairs * which + j))

    nbr, span, bias_w = bias.shape
    return pl.pallas_call(
        _dilated_body,
        grid=(b, pairs),
        in_specs=[spec(0), spec(1), spec(2),
                  pl.BlockSpec((nbr, span, bias_w // pairs), lambda i, j: (0, 0, j))],
        out_specs=pl.BlockSpec((None, seq, LANES), lambda i, j: (i, 0, j)),
        out_shape=jax.ShapeDtypeStruct((b, seq, GROUP_WIDTH), F32),
        scratch_shapes=[pltpu.VMEM((seq, LANES), F32)] * 3,
        compiler_params=_params(("parallel", "parallel")),
        name="dilated",
    )(zd, zd, zd, bias)


def _out_ffn_body(*refs, final):
    x_ref, oa_ref, ob_ref, oc_ref, od_ref, wo_ref, g_ref, wg_ref, wu_ref, wd_ref = refs[0:10]
    gf_ref = refs[10] if final else None
    o_ref = refs[-1]
    cat = jnp.concatenate([oa_ref[...], ob_ref[...], oc_ref[...], od_ref[...]], axis=1).astype(BF16)
    x = x_ref[...] + _dot(cat, wo_ref[...])
    x = _swiglu_residual(x, g_ref[...], wg_ref, wu_ref, wd_ref)
    if final:
        x = _rms(x, gf_ref[...])
    o_ref[...] = x


def _out_ffn(x2d, mixers, wo, gain, wg, wu, wd, gain_final, tm):
    m, d = x2d.shape
    tile = lambda w: pl.BlockSpec((tm, w), lambda i: (i, 0))
    final = gain_final is not None
    args = [x2d, *mixers, wo, gain, wg, wu, wd] + ([gain_final] if final else [])
    in_specs = ([tile(d)] + [tile(GROUP_WIDTH)] * 4
                + [_resident(wo.shape), _resident((1, d)), _resident(wg.shape), _resident(wu.shape),
                   _resident(wd.shape)] + ([_resident((1, d))] if final else []))
    return pl.pallas_call(
        functools.partial(_out_ffn_body, final=final),
        grid=(m // tm,),
        in_specs=in_specs,
        out_specs=tile(d),
        out_shape=jax.ShapeDtypeStruct((m, d), F32),
        compiler_params=_params(("parallel",)),
        name="out_ffn",
    )(*args)


def kernel(x, norm_ffn1, ffn1_gate, ffn1_up, ffn1_down, norm_mix, w_in, ckv_norm, w_kv_up, hgrn_lb_logits,
           hgrn_norm, w_out, norm_ffn2, ffn2_gate, ffn2_up, ffn2_down, rel_bias, norm_final):
    b, seq, d = x.shape
    depth = w_in.shape[0]
    nh = N_HEADS_GROUP
    m = b * seq
    tm = min(512, seq)
    topk = min(TOPK_MAX, seq // 4)

    toe_ids = _toeplitz_bucket_ids(seq)
    bias_a = _bias_tiles(rel_bias, toe_ids, 0)
    bias_b = _bias_tiles(rel_bias, toe_ids, nh)
    bias_d = _bias_tiles(rel_bias, _dilated_bucket_ids(), 2 * nh)

    x2d = x.reshape(m, d)
    for l in range(depth):
        x2d = _ffn(x2d, norm_ffn1[l][None], ffn1_gate[l].astype(BF16), ffn1_up[l].astype(BF16),
                   ffn1_down[l].astype(BF16), tm)
        za, zb, zc, zd = _inproj(x2d, norm_mix[l][None], _pack_w_in(w_in[l]), tm)
        za, zb, zc, zd = (z.reshape(b, seq, -1) for z in (za, zb, zc, zd))

        w_up = w_kv_up[l].reshape(KV_RANK, 2, nh, HEAD_DIM)
        wuk_t = jnp.transpose(w_up[:, 0], (1, 2, 0)).astype(BF16)
        wuv_t = jnp.transpose(w_up[:, 1], (1, 2, 0)).astype(BF16)
        mixers = [_dsa(za, ckv_norm[l][None], wuk_t, wuv_t, bias_a, topk), _moba(zb, bias_b),
                  _hgrn(zc, hgrn_lb_logits, hgrn_norm[l][None], l), _dilated(zd, bias_d)]
        mixers = [o.reshape(m, GROUP_WIDTH) for o in mixers]

        gain_final = norm_final[None] if l == depth - 1 else None
        x2d = _out_ffn(x2d, mixers, w_out[l].astype(BF16), norm_ffn2[l][None],
                       ffn2_gate[l].astype(BF16), ffn2_up[l].astype(BF16), ffn2_down[l].astype(BF16),
                       gain_final, tm)
    return x2d.reshape(b, seq, d)
```

```python
import functools
import math

import jax
import jax.numpy as jnp
import numpy as np
from jax import lax
from jax.experimental import pallas as pl
from jax.experimental.pallas import tpu as pltpu

F32 = jnp.float32
BF16 = jnp.bfloat16
I32 = jnp.int32

HEAD_DIM = 64
N_GROUPS = 4
GROUP_WIDTH = 256
N_HEADS_GROUP = GROUP_WIDTH // HEAD_DIM
ATTN_SCALE = HEAD_DIM ** -0.5
KV_RANK = 128
IDX_HEADS = 8
IDX_DIM = 32
TOPK_MAX = 256
MOBA_BLOCK = 256
MOBA_TOPK = 3
HGRN_CHUNK = 64
DILATED_BRANCHES = ((128, 1), (512, 4), (2048, 16))
DIL_SPAN = 128
N_BUCKETS = 32
MAX_DISTANCE = 2048
EPS = 1e-6

LANES = 128
VMEM_LIMIT_BYTES = 56 * 1024 * 1024

NEG = -1e30
INT_MIN = -2 ** 31

GA_W, GB_W, GC_W, GD_W = 768, 768, 1024, 768
A_IW_LANE = 32


def _dot(a, b):
    return jnp.dot(a, b, preferred_element_type=F32)


def _dot_nt(a, b):
    return lax.dot_general(a, b, (((1,), (1,)), ((), ())), preferred_element_type=F32)


def _dot_tn(a, b):
    return lax.dot_general(a, b, (((0,), (0,)), ((), ())), preferred_element_type=F32)


def _rms(x, gain):
    return x * lax.rsqrt(jnp.mean(x * x, axis=-1, keepdims=True) + EPS) * gain


def _split3(x):
    hi = x.astype(BF16)
    r1 = x - hi.astype(F32)
    mid = r1.astype(BF16)
    lo = (r1 - mid.astype(F32)).astype(BF16)
    return hi, mid, lo


def _params(sem):
    return pltpu.CompilerParams(dimension_semantics=sem, vmem_limit_bytes=VMEM_LIMIT_BYTES)


def _resident(shape):
    nd = len(shape)
    return pl.BlockSpec(shape, lambda *_: (0,) * nd, pipeline_mode=pl.Buffered(1))


def _bucket_np(dist):
    max_exact = N_BUCKETS // 2
    n = np.maximum(dist, 0)
    nf = np.maximum(n, 1).astype(np.float64)
    large = max_exact + (np.log(nf / max_exact) / math.log(MAX_DISTANCE / max_exact)
                         * (N_BUCKETS - max_exact)).astype(np.int64)
    large = np.minimum(large, N_BUCKETS - 1)
    return np.where(n < max_exact, n, large).astype(np.int32)


def _toeplitz_bucket_ids(seq):
    nj = seq // LANES
    s = np.arange(LANES)[:, None]
    q = np.arange(LANES)[None, :]
    j = np.arange(nj)[:, None, None]
    return _bucket_np(LANES * j + q - s)


def _dilated_bucket_ids():
    a = np.arange(DIL_SPAN)[:, None]
    c = np.arange(2 * DIL_SPAN)[None, :]
    delta = a + DIL_SPAN - c
    return np.stack([_bucket_np(delta * r) for _, r in DILATED_BRANCHES])


def _bias_tiles_body(tab_ref, ids_ref, o_ref, *, head0):
    ids = ids_ref[0]
    c = ids.shape[1]
    for h in range(N_HEADS_GROUP):
        acc = jnp.zeros(ids.shape, F32)
        for k in range(N_BUCKETS):
            acc = jnp.where(ids == k, tab_ref[k, head0 + h], acc)
        o_ref[0, :, h * c:(h + 1) * c] = acc


def _bias_tiles(rel_bias, ids, head0):
    n, r, c = ids.shape
    return pl.pallas_call(
        functools.partial(_bias_tiles_body, head0=head0),
        grid=(n,),
        in_specs=[pl.BlockSpec(memory_space=pltpu.SMEM),
                  pl.BlockSpec((1, r, c), lambda i: (i, 0, 0))],
        out_specs=pl.BlockSpec((1, r, N_HEADS_GROUP * c), lambda i: (i, 0, 0)),
        out_shape=jax.ShapeDtypeStruct((n, r, N_HEADS_GROUP * c), F32),
        compiler_params=_params(("arbitrary",)),
        name="bias_tiles",
    )(rel_bias, jnp.asarray(ids))


FFN_CHUNK = 256
FFN_TOKENS = 1024


def _swiglu_residual(x, gain, wg_ref, wu_ref, wd_ref):
    h = _rms(x, gain).astype(BF16)
    d_ff = wg_ref.shape[1]
    hidden = []
    for c0 in range(0, d_ff, FFN_CHUNK):
        g = _dot(h, wg_ref[:, c0:c0 + FFN_CHUNK])
        u = _dot(h, wu_ref[:, c0:c0 + FFN_CHUNK])
        hidden.append((g * jax.nn.sigmoid(g) * u).astype(BF16))
    return x + 0.5 * _dot(jnp.concatenate(hidden, axis=1), wd_ref[...])


def _ffn_body(x_ref, g_ref, wg_ref, wu_ref, wd_ref, o_ref):
    o_ref[...] = _swiglu_residual(x_ref[...], g_ref[...], wg_ref, wu_ref, wd_ref)


def _ffn(x2d, gain, wg, wu, wd, tm):
    m, d = x2d.shape
    return pl.pallas_call(
        _ffn_body,
        grid=(m // tm,),
        in_specs=[pl.BlockSpec((tm, d), lambda i: (i, 0)),
                  _resident((1, d)), _resident(wg.shape), _resident(wu.shape), _resident(wd.shape)],
        out_specs=pl.BlockSpec((tm, d), lambda i: (i, 0)),
        out_shape=jax.ShapeDtypeStruct((m, d), F32),
        compiler_params=_params(("parallel",)),
        name="ffn",
    )(x2d, gain, wg, wu, wd)


def _inproj_body(x_ref, g_ref, w_ref, za_ref, zb_ref, zc_ref, zd_ref):
    h = _rms(x_ref[...], g_ref[...]).astype(BF16)
    c0 = 0
    for ref in (za_ref, zb_ref, zc_ref, zd_ref):
        w = ref.shape[1]
        ref[...] = _dot(h, w_ref[:, c0:c0 + w])
        c0 += w


def _inproj(x2d, gain, w_packed, tm):
    m, d = x2d.shape
    widths = (GA_W, GB_W, GC_W, GD_W)
    return pl.pallas_call(
        _inproj_body,
        grid=(m // tm,),
        in_specs=[pl.BlockSpec((tm, d), lambda i: (i, 0)), _resident((1, d)), _resident(w_packed.shape)],
        out_specs=[pl.BlockSpec((tm, w), lambda i: (i, 0)) for w in widths],
        out_shape=[jax.ShapeDtypeStruct((m, w), F32) for w in widths],
        compiler_params=_params(("parallel",)),
        name="inproj",
    )(x2d, gain, w_packed)


def _pack_w_in(w):
    d = w.shape[0]
    pad = jnp.zeros((d, GA_W - 680), w.dtype)
    return jnp.concatenate([w[:, 0:256], w[:, 384:640], w[:, 256:384], w[:, 640:680], pad, w[:, 680:]],
                           axis=1).astype(BF16)


def _online_softmax_step(s, vt_bf16, m, l, acc):
    m_new = jnp.maximum(m, jnp.max(s, axis=0, keepdims=True))
    alpha = jnp.exp(m - m_new)
    p = jnp.exp(s - m_new)
    l = alpha * l + jnp.sum(p, axis=0, keepdims=True)
    acc = alpha * acc + _dot(vt_bf16, p.astype(BF16))
    return m_new, l, acc


DSA_TQ = 256
DSA_TK = LANES
DSA_TILE_PAD = 4
DSA_SCORE_GROUP = 4
DSA_COUNT_GROUP = 4
DSA_ATTN_GROUP = 4


def _dsa_body(q_ref, iq_ref, iwq_ref, ckv_ref, ikw_ref, gain_ref, wuk_ref, wuvt_ref, bias_ref,
              o_ref, c_sc, ct_sc, ik_sc, key_sc, *, topk):
    tq, tk = DSA_TQ, DSA_TK
    n = pl.program_id(1)
    seq = ckv_ref.shape[0]
    nheads = N_HEADS_GROUP
    lane = lax.broadcasted_iota(I32, (1, LANES), 1)

    @pl.when(n == 0)
    def _():
        for r0 in range(0, seq, tk):
            c = _rms(ckv_ref[r0:r0 + tk, :], gain_ref[...])
            c_sc[r0:r0 + tk, :] = c.astype(BF16)
            ct_sc[:, r0:r0 + tk] = c.T.astype(BF16)
            ik = jnp.where(lane < IDX_DIM, ikw_ref[r0:r0 + tk, :], 0.0)
            ik4 = ik
            for rep in range(1, LANES // IDX_DIM):
                ik4 = ik4 + pltpu.roll(ik, rep * IDX_DIM, 1)
            ik_sc[r0:r0 + tk, :] = ik4.astype(BF16)

    iq = iq_ref[...]
    iw_t = (iwq_ref[...] * (IDX_HEADS * IDX_DIM) ** -0.5).T
    tiles_per_q = tq // tk
    heads_per_group = LANES // IDX_DIM
    lhs = []
    for h in range(IDX_HEADS):
        blk = iq[:, LANES * (h // heads_per_group):LANES * (h // heads_per_group + 1)]
        lo = IDX_DIM * (h % heads_per_group)
        lhs.append(jnp.where((lane >= lo) & (lane < lo + IDX_DIM), blk, 0.0).astype(BF16))
    lhs = jnp.concatenate(lhs, axis=0)
    qpos = n * tq + lax.broadcasted_iota(I32, (1, tq), 1)
    srow = lax.broadcasted_iota(I32, (tk, 1), 0)
    pad_shift = DSA_TILE_PAD.bit_length() - 1
    ntiles = lax.shift_left(
        lax.shift_right_logical((n + 1) * tiles_per_q + DSA_TILE_PAD - 1, pad_shift), pad_shift)

    def over_key_tiles(group, body, carry):
        ngroups = lax.shift_right_logical(ntiles, group.bit_length() - 1)
        return lax.fori_loop(0, ngroups, lambda i, c: body(i * group, group, c), carry)

    def score_tiles(kt, g, carry):
        k0 = pl.multiple_of(kt * tk, tk)
        s = _dot_nt(ik_sc[pl.ds(k0, g * tk), :], lhs)
        idx = jnp.zeros((g * tk, tq), F32)
        for h in range(IDX_HEADS):
            idx = idx + iw_t[A_IW_LANE + h:A_IW_LANE + h + 1, :] * jnp.maximum(s[:, h * tq:(h + 1) * tq], 0.0)
        bits = pltpu.bitcast(idx, I32)
        keys = bits ^ ((bits >> 31) & 0x7FFFFFFF)
        spos = k0 + lax.broadcasted_iota(I32, (g * tk, 1), 0)
        key_sc[pl.ds(k0, g * tk), :] = jnp.where(spos <= qpos, keys, INT_MIN)
        return carry

    over_key_tiles(DSA_SCORE_GROUP, score_tiles, 0)

    def count(pred):
        def body(kt, g, acc):
            for i in range(g):
                k0 = pl.multiple_of((kt + i) * tk, tk)
                acc = acc + jnp.where(pred(key_sc[pl.ds(k0, tk), :], k0), 1, 0)
            return acc
        acc = over_key_tiles(DSA_COUNT_GROUP, body, jnp.zeros((tk, tq), I32))
        return jnp.sum(acc, axis=0, keepdims=True)

    def bit_iter(i, t):
        cand = t | lax.shift_left(jnp.int32(1), 31 - i)
        cand_s = cand ^ INT_MIN
        cnt = count(lambda keys, k0: keys >= cand_s)
        return jnp.where(cnt >= topk, cand, t)

    t = lax.fori_loop(0, 32, bit_iter, jnp.zeros((1, tq), I32))
    tau = t ^ INT_MIN
    cnt_gt = count(lambda keys, k0: keys > tau)
    cnt_eq = count(lambda keys, k0: keys == tau)
    need = topk - cnt_gt
    tie_q = (cnt_eq > need) & (t != 0)

    @pl.when(jnp.max(tie_q.astype(I32)) > 0)
    def _():
        def pos_iter(i, p):
            cand = p | lax.shift_left(jnp.int32(1), (seq.bit_length() - 1) - i)
            cnt = count(lambda keys, k0: (keys == tau) & (k0 + srow < cand))
            return jnp.where(cnt < need, cand, p)
        p = lax.fori_loop(0, seq.bit_length(), pos_iter, jnp.zeros((1, tq), I32))
        p = jnp.where(tie_q, p, seq)

        def drop(kt, carry):
            k0 = pl.multiple_of(kt * tk, tk)
            keys = key_sc[pl.ds(k0, tk), :]
            key_sc[pl.ds(k0, tk), :] = jnp.where((keys == tau) & (k0 + srow > p), INT_MIN, keys)
            return carry
        lax.fori_loop(0, ntiles, drop, 0)

    tau_sel = jnp.maximum(tau, INT_MIN + 1)

    q = q_ref[...]
    qs = []
    for h in range(nheads):
        qh = q[:, h * HEAD_DIM:(h + 1) * HEAD_DIM].astype(BF16)
        qs.append((_dot(qh, wuk_ref[h]) * ATTN_SCALE).astype(BF16))
    qs = jnp.concatenate(qs, axis=0)
    cols = nheads * tq

    def bias_rows(kt):
        parts = [bias_ref[jnp.maximum(n * tiles_per_q + qa - kt, 0)] for qa in range(tiles_per_q)]
        return jnp.concatenate([p[:, h * LANES:(h + 1) * LANES] for h in range(nheads) for p in parts], axis=1)

    def attn_tiles(kt, g, carry):
        m, l, acc = carry
        k0 = pl.multiple_of(kt * tk, tk)
        s = _dot_nt(c_sc[pl.ds(k0, g * tk), :], qs)
        drop = jnp.where(key_sc[pl.ds(k0, g * tk), :] >= tau_sel, 0.0, NEG)
        bias = jnp.concatenate([bias_rows(kt + i) for i in range(g)], axis=0)
        s = s + bias + jnp.concatenate([drop] * nheads, axis=1)
        return _online_softmax_step(s, ct_sc[:, pl.ds(k0, g * tk)], m, l, acc)

    m0 = jnp.full((1, cols), NEG, F32)
    l0 = jnp.zeros((1, cols), F32)
    acc0 = jnp.zeros((KV_RANK, cols), F32)
    m, l, acc = over_key_tiles(DSA_ATTN_GROUP, attn_tiles, (m0, l0, acc0))
    ctx = (acc / l).astype(BF16)
    out_t = jnp.concatenate([_dot(wuvt_ref[h], ctx[:, h * tq:(h + 1) * tq]) for h in range(nheads)], axis=0)
    o_ref[...] = out_t.T


def _dsa(za, ckv_gain, wuk_t, wuv_t, bias, topk):
    b, seq, _ = za.shape
    tq = DSA_TQ
    assert seq % tq == 0 and seq % (DSA_TK * DSA_TILE_PAD) == 0
    return pl.pallas_call(
        functools.partial(_dsa_body, topk=topk),
        grid=(b, seq // tq),
        in_specs=[pl.BlockSpec((None, tq, 256), lambda i, j: (i, j, 0)),
                  pl.BlockSpec((None, tq, 256), lambda i, j: (i, j, 1)),
                  pl.BlockSpec((None, tq, LANES), lambda i, j: (i, j, 5)),
                  pl.BlockSpec((None, seq, LANES), lambda i, j: (i, 0, 4)),
                  pl.BlockSpec((None, seq, LANES), lambda i, j: (i, 0, 5)),
                  _resident(ckv_gain.shape), _resident(wuk_t.shape), _resident(wuv_t.shape),
                  _resident(bias.shape)],
        out_specs=pl.BlockSpec((None, tq, GROUP_WIDTH), lambda i, j: (i, j, 0)),
        out_shape=jax.ShapeDtypeStruct((b, seq, GROUP_WIDTH), F32),
        scratch_shapes=[pltpu.VMEM((seq, KV_RANK), BF16), pltpu.VMEM((KV_RANK, seq), BF16),
                        pltpu.VMEM((seq, LANES), BF16), pltpu.VMEM((seq, tq), I32)],
        compiler_params=_params(("parallel", "arbitrary")),
        name="dsa",
    )(za, za, za, za, za, ckv_gain, wuk_t, wuv_t, bias)


MOBA_GROUP = 2


def _moba_bias_tile(bias_ref, d):
    lo = jnp.maximum(2 * d - 1, 0)
    t_same, t_next, t_prev = bias_ref[2 * d], bias_ref[2 * d + 1], bias_ref[lo]
    top, bot = [], []
    for h in range(N_HEADS_GROUP):
        hs = slice(h * LANES, (h + 1) * LANES)
        top += [t_same[:, hs], t_next[:, hs]]
        bot += [t_prev[:, hs], t_same[:, hs]]
    return jnp.concatenate([jnp.concatenate(top, axis=1), jnp.concatenate(bot, axis=1)], axis=0)


def _moba_body(q_ref, k_ref, v_ref, bias_ref, o_ref, kh_sc, vt_sc, km_sc, sel_sc, *, topb):
    blk = MOBA_BLOCK
    n = pl.program_id(1)
    seq = k_ref.shape[0]
    nb = seq // blk
    nheads = N_HEADS_GROUP

    @pl.when(n == 0)
    def _():
        for j in range(nb):
            kb = k_ref[j * blk:(j + 1) * blk, :]
            vbt = v_ref[j * blk:(j + 1) * blk, :].T
            km = jnp.mean(kb, axis=0, keepdims=True)
            for h in range(nheads):
                sl = slice(h * HEAD_DIM, (h + 1) * HEAD_DIM)
                kh_sc[h, j * blk:(j + 1) * blk, :] = kb[:, sl].astype(BF16)
                vt_sc[h, :, j * blk:(j + 1) * blk] = vbt[sl, :].astype(BF16)
                km_sc[h, j:j + 1, :] = km[:, sl]

    q = q_ref[...]
    rowid = lax.broadcasted_iota(I32, (nb, blk), 0)
    qs = []
    for h in range(nheads):
        qh = q[:, h * HEAD_DIM:(h + 1) * HEAD_DIM]
        km_hi, km_mid, _ = _split3(km_sc[h])
        q_hi, q_mid, _ = _split3(qh)
        gate = _dot_nt(km_hi, q_hi) + _dot_nt(km_mid, q_hi) + _dot_nt(km_hi, q_mid)
        gate = jnp.where(rowid < n, gate, -jnp.inf)
        rank = jnp.zeros((nb, blk), I32)
        for mth in range(nb):
            gm = gate[mth:mth + 1, :]
            beats = (gm > gate) | ((gm == gate) & (mth < rowid))
            rank = rank + jnp.where(beats, 1, 0)
        sel_sc[:, h * blk:(h + 1) * blk] = jnp.where((rank < topb) & (rowid < n), 0.0, NEG)
        qs.append((qh * ATTN_SCALE).astype(BF16))

    def logits(j0, d):
        s = jnp.concatenate([_dot_nt(kh_sc[h, pl.ds(j0, blk), :], qs[h]) for h in range(nheads)], axis=1)
        return s + _moba_bias_tile(bias_ref, d)

    def weighted_values(j0, p):
        p = p.astype(BF16)
        return jnp.concatenate([_dot(vt_sc[h, :, pl.ds(j0, blk)], p[:, h * blk:(h + 1) * blk])
                                for h in range(nheads)], axis=1)

    k0 = pl.multiple_of(n * blk, blk)
    wide = (blk, nheads * blk)
    causal = lax.broadcasted_iota(I32, wide, 0) <= (lax.broadcasted_iota(I32, wide, 1) & (blk - 1))
    s = jnp.where(causal, logits(k0, 0), NEG)
    m = jnp.max(s, axis=0, keepdims=True)
    p = jnp.exp(s - m)
    l = jnp.sum(p, axis=0, keepdims=True)
    acc = weighted_values(k0, p)

    def past_blocks(i, carry):
        m, l, acc = carry
        js = [i * MOBA_GROUP + g for g in range(MOBA_GROUP)]
        j0s = [pl.multiple_of(j * blk, blk) for j in js]
        s = jnp.concatenate([logits(j0, n - j) + sel_sc[pl.ds(j, 1), :] for j, j0 in zip(js, j0s)], axis=0)
        m_new = jnp.maximum(m, jnp.max(s, axis=0, keepdims=True))
        alpha = jnp.exp(m - m_new)
        p = jnp.exp(s - m_new)
        l = alpha * l + jnp.sum(p, axis=0, keepdims=True)
        acc = alpha * acc
        for g, j0 in enumerate(j0s):
            acc = acc + weighted_values(j0, p[g * blk:(g + 1) * blk])
        return m_new, l, acc

    ntrips = lax.shift_right_logical(n + MOBA_GROUP - 1, MOBA_GROUP.bit_length() - 1)
    m, l, acc = lax.fori_loop(0, ntrips, past_blocks, (m, l, acc))
    out = acc / l
    out_t = jnp.concatenate([out[:, h * blk:(h + 1) * blk] for h in range(nheads)], axis=0)
    o_ref[...] = out_t.T


def _moba(zb, bias):
    b, seq, _ = zb.shape
    blk = MOBA_BLOCK
    nb = seq // blk
    return pl.pallas_call(
        functools.partial(_moba_body, topb=min(MOBA_TOPK, nb)),
        grid=(b, nb),
        in_specs=[pl.BlockSpec((None, blk, GROUP_WIDTH), lambda i, j: (i, j, 0)),
                  pl.BlockSpec((None, seq, GROUP_WIDTH), lambda i, j: (i, 0, 1)),
                  pl.BlockSpec((None, seq, GROUP_WIDTH), lambda i, j: (i, 0, 2)),
                  _resident(bias.shape)],
        out_specs=pl.BlockSpec((None, blk, GROUP_WIDTH), lambda i, j: (i, j, 0)),
        out_shape=jax.ShapeDtypeStruct((b, seq, GROUP_WIDTH), F32),
        scratch_shapes=[pltpu.VMEM((N_HEADS_GROUP, seq, HEAD_DIM), BF16),
                        pltpu.VMEM((N_HEADS_GROUP, HEAD_DIM, seq), BF16),
                        pltpu.VMEM((N_HEADS_GROUP, nb, HEAD_DIM), F32),
                        pltpu.VMEM((nb, N_HEADS_GROUP * blk), F32)],
        compiler_params=_params(("parallel", "arbitrary")),
        name="moba",
    )(zb, zb, zb, bias)


HGRN_STEP = 256
_HGRN_SUBS = (16, 4, 1)
_HGRN_FANOUT = 4


def _hgrn_constants():
    c = HGRN_CHUNK
    fan = _HGRN_FANOUT
    t = np.arange(c)[:, None]
    u = np.arange(c)[None, :]
    sel = [(u <= t),
           (u > t),
           np.ones((8, c), bool)]
    masks = []
    for sub in _HGRN_SUBS:
        pos = (t // sub) % fan
        if sub > 1:
            e_own = (t // sub) * sub + sub - 1
            sel.append((u > t) & (u <= e_own))
        for j in range(fan - 1):
            e_j = (t // (fan * sub)) * fan * sub + j * sub + sub - 1
            later = pos > j
            sel.append(later & (u > e_j) & (u <= t))
            s = u
            same_blk = (t // (fan * sub)) == (s // (fan * sub))
            masks.append(same_blk & later & ((s // sub) % fan == j))
    masks.append(t == u)
    sel = np.concatenate(sel, axis=0).astype(np.float32)
    masks = np.stack([np.tile(mk, (HGRN_STEP // c, N_HEADS_GROUP)) for mk in masks]).astype(np.float32)
    hid = np.arange(GROUP_WIDTH) // HEAD_DIM
    blockdiag = (hid[:, None] == hid[None, :]).astype(np.float32)
    return sel, masks, blockdiag


def _hgrn_body(z_ref, lbl_ref, gain_ref, sel_ref, mask_ref, bd_ref, o_ref, st_sc, *, layer):
    c = HGRN_CHUNK
    w = GROUP_WIDTH
    nheads = N_HEADS_GROUP

    @pl.when(pl.program_id(1) == 0)
    def _():
        st_sc[...] = jnp.zeros(st_sc.shape, F32)

    logits = lbl_ref[...]
    e = jnp.exp(logits - jnp.max(logits, axis=0, keepdims=True))
    lb_w = e / jnp.sum(e, axis=0, keepdims=True)
    lb = jnp.sum(lb_w[0:layer + 1], axis=0, keepdims=True) - lb_w[0:1]

    bd = bd_ref[...] > 0.5
    bd_bf16 = bd_ref[...].astype(BF16)
    sel = sel_ref[...].astype(BF16)

    def blockdiag(x_bf16):
        return jnp.where(bd, jnp.concatenate([x_bf16] * nheads, axis=0), jnp.zeros((), BF16))

    chunks = [slice(r0, r0 + c) for r0 in range(0, z_ref.shape[0], c)]

    def per_chunk(fn):
        return jnp.concatenate([fn(rc) for rc in chunks], axis=0)

    q = z_ref[:, 0:w]
    f = lb + (1.0 - lb) * jax.nn.sigmoid(z_ref[:, w:2 * w])
    kin = 1.0 - f
    hi, mid, lo = _split3(jnp.log(f))
    sel3 = jnp.concatenate([sel] * 3, axis=1)
    decays = [jnp.exp(_dot(sel3, jnp.concatenate([hi[rc], mid[rc], lo[rc]], axis=0))) for rc in chunks]

    def decay_rows(base):
        return jnp.concatenate([d[base:base + c] for d in decays], axis=0)

    a = jnp.zeros(q.shape, F32)
    base = 2 * c + 8
    mi = 0
    for sub in _HGRN_SUBS:
        if sub > 1:
            km = (kin * decay_rows(base)).astype(BF16)
            base += c
        else:
            km = kin.astype(BF16)
        variants = [(q * decay_rows(base + j * c)).astype(BF16) for j in range(_HGRN_FANOUT - 1)]
        base += (_HGRN_FANOUT - 1) * c
        if sub == 1:
            variants.append(q.astype(BF16))
        prods = [_dot_nt(jnp.concatenate([v[rc] for v in variants], axis=0), blockdiag(km[rc])) for rc in chunks]
        for j in range(len(variants)):
            a = a + mask_ref[mi] * jnp.concatenate([p[j * c:(j + 1) * c] for p in prods], axis=0)
            mi += 1

    a_bf16 = a.astype(BF16)
    v_bf16 = z_ref[:, 2 * w:3 * w].astype(BF16)
    o = per_chunk(lambda rc: _dot(a_bf16[rc], blockdiag(v_bf16[rc])))
    khat = (kin * decay_rows(c)).astype(BF16)
    deltas = [jnp.where(bd, _dot_tn(v_bf16[rc], khat[rc]), 0.0) for rc in chunks]
    q_dec = (q * decay_rows(0)).astype(BF16)
    st = st_sc[...]
    carried = []
    for ci, rc in enumerate(chunks):
        carried.append(_dot_nt(q_dec[rc], st.astype(BF16)))
        st = st * decays[ci][2 * c:2 * c + 1] + deltas[ci]
    st_sc[...] = st
    o = o + jnp.concatenate(carried, axis=0)

    oo = o * o
    oo_hi = oo.astype(BF16)
    oo_lo = (oo - oo_hi.astype(F32)).astype(BF16)
    ms = (_dot(oo_hi, bd_bf16) + _dot(oo_lo, bd_bf16)) * (1.0 / HEAD_DIM)
    g = z_ref[:, 3 * w:4 * w]
    o_ref[...] = o * lax.rsqrt(ms + EPS) * gain_ref[...] * (g * jax.nn.sigmoid(g))


def _hgrn(zc, lb_logits, gain, layer):
    b, seq, _ = zc.shape
    sel, masks, blockdiag = _hgrn_constants()
    return pl.pallas_call(
        functools.partial(_hgrn_body, layer=layer),
        grid=(b, seq // HGRN_STEP),
        in_specs=[pl.BlockSpec((None, HGRN_STEP, GC_W), lambda i, j: (i, j, 0)),
                  _resident(lb_logits.shape), _resident(gain.shape),
                  _resident(sel.shape), _resident(masks.shape), _resident(blockdiag.shape)],
        out_specs=pl.BlockSpec((None, HGRN_STEP, GROUP_WIDTH), lambda i, j: (i, j, 0)),
        out_shape=jax.ShapeDtypeStruct((b, seq, GROUP_WIDTH), F32),
        scratch_shapes=[pltpu.VMEM((GROUP_WIDTH, GROUP_WIDTH), F32)],
        compiler_params=_params(("parallel", "arbitrary")),
        name="hgrn",
    )(zc, lb_logits, gain, jnp.asarray(sel), jnp.asarray(masks), jnp.asarray(blockdiag))


DIL_GROUP = 4


def _dilated_body(q_ref, k_ref, v_ref, bias_ref, o_ref, m_sc, l_sc, acc_sc):
    span = DIL_SPAN
    seq = q_ref.shape[0]
    lane = lax.broadcasted_iota(I32, (1, LANES), 1)
    head_lanes = [lane < HEAD_DIM, lane >= HEAD_DIM]
    a_i = lax.broadcasted_iota(I32, (span, 2 * span), 0)
    c_i = lax.broadcasted_iota(I32, (span, 2 * span), 1)
    delta = a_i + span - c_i
    window = (delta >= 0) & (delta <= span)
    half_ones = [jnp.where(lane < HEAD_DIM, 1.0, 0.0).astype(BF16) * jnp.ones((2 * span, 1), BF16),
                 jnp.where(lane >= HEAD_DIM, 1.0, 0.0).astype(BF16) * jnp.ones((2 * span, 1), BF16)]
    last = len(DILATED_BRANCHES) - 1

    for bi, (_, r) in enumerate(DILATED_BRANCHES):
        shift = r.bit_length() - 1

        def rows(start, r=r):
            return pl.ds(start, span) if r == 1 else pl.ds(start, span, stride=r)

        def blocks(i, carry, bi=bi, r=r, shift=shift, rows=rows):
            grp = range(DIL_GROUP)
            starts, q, kcat, vcat, mask = [], [], [], [], []
            for g in grp:
                blk = i * DIL_GROUP + g
                j = blk & (r - 1)
                n = lax.shift_right_logical(blk, shift)
                start = j + r * span * n
                prev = jnp.maximum(start - r * span, j)
                starts.append(start)
                q.append(q_ref[rows(start), :] * ATTN_SCALE)
                kcat.append(jnp.concatenate([k_ref[rows(prev), :], k_ref[rows(start), :]], axis=0).astype(BF16))
                vcat.append(jnp.concatenate([v_ref[rows(prev), :], v_ref[rows(start), :]], axis=0))
                mask.append(window & ((c_i >= span) | (n > 0)))
            mask = jnp.concatenate([jnp.where(mk, 0.0, NEG) for mk in mask], axis=0)
            num = [jnp.zeros((span, LANES), F32) for _ in grp]
            den = [jnp.zeros((span, LANES), F32) for _ in grp]
            m_b = jnp.zeros((DIL_GROUP * span, LANES), F32)
            for h in range(2):
                bias = bias_ref[bi, :, h * 2 * span:(h + 1) * 2 * span]
                logits = jnp.concatenate(
                    [_dot_nt(jnp.where(head_lanes[h], q[g], 0.0).astype(BF16), kcat[g]) + bias for g in grp], axis=0)
                logits = logits + mask
                m = jnp.max(logits, axis=1, keepdims=True)
                p = jnp.exp(logits - m).astype(BF16)
                m_b = jnp.where(head_lanes[h], m, m_b)
                for g in grp:
                    pg = p[g * span:(g + 1) * span]
                    num[g] = num[g] + _dot(pg, jnp.where(head_lanes[h], vcat[g], 0.0).astype(BF16))
                    den[g] = den[g] + _dot(pg, half_ones[h])
            num = jnp.concatenate(num, axis=0)
            den = jnp.concatenate(den, axis=0)
            if bi == 0:
                m_new, l_new, acc_new = m_b, den, num
            else:
                m_old = jnp.concatenate([m_sc[rows(st), :] for st in starts], axis=0)
                l_old = jnp.concatenate([l_sc[rows(st), :] for st in starts], axis=0)
                acc_old = jnp.concatenate([acc_sc[rows(st), :] for st in starts], axis=0)
                m_new = jnp.maximum(m_old, m_b)
                w_old = jnp.exp(m_old - m_new)
                w_b = jnp.exp(m_b - m_new)
                l_new = w_old * l_old + w_b * den
                acc_new = w_old * acc_old + w_b * num
            out = acc_new / l_new if bi == last else None
            for g, st in enumerate(starts):
                sl = slice(g * span, (g + 1) * span)
                if bi == last:
                    o_ref[rows(st), :] = out[sl]
                else:
                    m_sc[rows(st), :] = m_new[sl]
                    l_sc[rows(st), :] = l_new[sl]
                    acc_sc[rows(st), :] = acc_new[sl]
            return carry

        lax.fori_loop(0, seq // (span * DIL_GROUP), blocks, 0)


def _dilated(zd, bias):
    b, seq, _ = zd.shape
    pairs = GROUP_WIDTH // LANES
    assert seq % max(w for w, _ in DILATED_BRANCHES) == 0

    def spec(which):
        return pl.BlockSpec((None, seq, LANES), lambda i, j: (i, 0, pairs * which + j))

    nbr, span, bias_w = bias.shape
    return pl.pallas_call(
        _dilated_body,
        grid=(b, pairs),
        in_specs=[spec(0), spec(1), spec(2),
                  pl.BlockSpec((nbr, span, bias_w // pairs), lambda i, j: (0, 0, j))],
        out_specs=pl.BlockSpec((None, seq, LANES), lambda i, j: (i, 0, j)),
        out_shape=jax.ShapeDtypeStruct((b, seq, GROUP_WIDTH), F32),
        scratch_shapes=[pltpu.VMEM((seq, LANES), F32)] * 3,
        compiler_params=_params(("parallel", "parallel")),
        name="dilated",
    )(zd, zd, zd, bias)


def _out_ffn_body(*refs, final):
    x_ref, oa_ref, ob_ref, oc_ref, od_ref, wo_ref, g_ref, wg_ref, wu_ref, wd_ref = refs[0:10]
    gf_ref = refs[10] if final else None
    o_ref = refs[-1]
    cat = jnp.concatenate([oa_ref[...], ob_ref[...], oc_ref[...], od_ref[...]], axis=1).astype(BF16)
    x = x_ref[...] + _dot(cat, wo_ref[...])
    x = _swiglu_residual(x, g_ref[...], wg_ref, wu_ref, wd_ref)
    if final:
        x = _rms(x, gf_ref[...])
    o_ref[...] = x


def _out_ffn(x2d, mixers, wo, gain, wg, wu, wd, gain_final, tm):
    m, d = x2d.shape
    tile = lambda w: pl.BlockSpec((tm, w), lambda i: (i, 0))
    final = gain_final is not None
    args = [x2d, *mixers, wo, gain, wg, wu, wd] + ([gain_final] if final else [])
    in_specs = ([tile(d)] + [tile(GROUP_WIDTH)] * 4
                + [_resident(wo.shape), _resident((1, d)), _resident(wg.shape), _resident(wu.shape),
                   _resident(wd.shape)] + ([_resident((1, d))] if final else []))
    return pl.pallas_call(
        functools.partial(_out_ffn_body, final=final),
        grid=(m // tm,),
        in_specs=in_specs,
        out_specs=tile(d),
        out_shape=jax.ShapeDtypeStruct((m, d), F32),
        compiler_params=_params(("parallel",)),
        name="out_ffn",
    )(*args)


def kernel(x, norm_ffn1, ffn1_gate, ffn1_up, ffn1_down, norm_mix, w_in, ckv_norm, w_kv_up, hgrn_lb_logits,
           hgrn_norm, w_out, norm_ffn2, ffn2_gate, ffn2_up, ffn2_down, rel_bias, norm_final):
    b, seq, d = x.shape
    depth = w_in.shape[0]
    nh = N_HEADS_GROUP
    m = b * seq
    tm = min(512, seq)
    topk = min(TOPK_MAX, seq // 4)

    toe_ids = _toeplitz_bucket_ids(seq)
    bias_a = _bias_tiles(rel_bias, toe_ids, 0)
    bias_b = _bias_tiles(rel_bias, toe_ids, nh)
    bias_d = _bias_tiles(rel_bias, _dilated_bucket_ids(), 2 * nh)

    x2d = x.reshape(m, d)
    for l in range(depth):
        x2d = _ffn(x2d, norm_ffn1[l][None], ffn1_gate[l].astype(BF16), ffn1_up[l].astype(BF16),
                   ffn1_down[l].astype(BF16), min(FFN_TOKENS, m))
        za, zb, zc, zd = _inproj(x2d, norm_mix[l][None], _pack_w_in(w_in[l]), tm)
        za, zb, zc, zd = (z.reshape(b, seq, -1) for z in (za, zb, zc, zd))

        w_up = w_kv_up[l].reshape(KV_RANK, 2, nh, HEAD_DIM)
        wuk_t = jnp.transpose(w_up[:, 0], (1, 2, 0)).astype(BF16)
        wuv_t = jnp.transpose(w_up[:, 1], (1, 2, 0)).astype(BF16)
        mixers = [_dsa(za, ckv_norm[l][None], wuk_t, wuv_t, bias_a, topk), _moba(zb, bias_b),
                  _hgrn(zc, hgrn_lb_logits, hgrn_norm[l][None], l), _dilated(zd, bias_d)]
        mixers = [o.reshape(m, GROUP_WIDTH) for o in mixers]

        gain_final = norm_final[None] if l == depth - 1 else None
        x2d = _out_ffn(x2d, mixers, w_out[l].astype(BF16), norm_ffn2[l][None],
                       ffn2_gate[l].astype(BF16), ffn2_up[l].astype(BF16), ffn2_down[l].astype(BF16),
                       gain_final, tm)
    return x2d.reshape(b, seq, d)
```

```python
import functools
import math

import jax
import jax.numpy as jnp
import numpy as np
from jax import lax
from jax.experimental import pallas as pl
from jax.experimental.pallas import tpu as pltpu

F32 = jnp.float32
BF16 = jnp.bfloat16
I32 = jnp.int32

HEAD_DIM = 64
N_GROUPS = 4
GROUP_WIDTH = 256
N_HEADS_GROUP = GROUP_WIDTH // HEAD_DIM
ATTN_SCALE = HEAD_DIM ** -0.5
KV_RANK = 128
IDX_HEADS = 8
IDX_DIM = 32
TOPK_MAX = 256
MOBA_BLOCK = 256
MOBA_TOPK = 3
HGRN_CHUNK = 64
DILATED_BRANCHES = ((128, 1), (512, 4), (2048, 16))
DIL_SPAN = 128
N_BUCKETS = 32
MAX_DISTANCE = 2048
EPS = 1e-6

LANES = 128
VMEM_LIMIT_BYTES = 56 * 1024 * 1024

NEG = -1e30
INT_MIN = -2 ** 31

GA_W, GB_W, GC_W, GD_W = 768, 768, 1024, 768
A_IW_LANE = 32


def _dot(a, b):
    return jnp.dot(a, b, preferred_element_type=F32)


def _dot_nt(a, b):
    return lax.dot_general(a, b, (((1,), (1,)), ((), ())), preferred_element_type=F32)


def _dot_tn(a, b):
    return lax.dot_general(a, b, (((0,), (0,)), ((), ())), preferred_element_type=F32)


def _rms(x, gain):
    return x * lax.rsqrt(jnp.mean(x * x, axis=-1, keepdims=True) + EPS) * gain


def _split3(x):
    hi = x.astype(BF16)
    r1 = x - hi.astype(F32)
    mid = r1.astype(BF16)
    lo = (r1 - mid.astype(F32)).astype(BF16)
    return hi, mid, lo


def _params(sem):
    return pltpu.CompilerParams(dimension_semantics=sem, vmem_limit_bytes=VMEM_LIMIT_BYTES)


def _resident(shape):
    nd = len(shape)
    return pl.BlockSpec(shape, lambda *_: (0,) * nd, pipeline_mode=pl.Buffered(1))


def _bucket_np(dist):
    max_exact = N_BUCKETS // 2
    n = np.maximum(dist, 0)
    nf = np.maximum(n, 1).astype(np.float64)
    large = max_exact + (np.log(nf / max_exact) / math.log(MAX_DISTANCE / max_exact)
                         * (N_BUCKETS - max_exact)).astype(np.int64)
    large = np.minimum(large, N_BUCKETS - 1)
    return np.where(n < max_exact, n, large).astype(np.int32)


def _toeplitz_bucket_ids(seq):
    nj = seq // LANES
    s = np.arange(LANES)[:, None]
    q = np.arange(LANES)[None, :]
    j = np.arange(nj)[:, None, None]
    return _bucket_np(LANES * j + q - s)


def _dilated_bucket_ids():
    a = np.arange(DIL_SPAN)[:, None]
    c = np.arange(2 * DIL_SPAN)[None, :]
    delta = a + DIL_SPAN - c
    return np.stack([_bucket_np(delta * r) for _, r in DILATED_BRANCHES])


def _bias_tiles_body(tab_ref, ids_ref, o_ref, *, head0):
    ids = ids_ref[0]
    c = ids.shape[1]
    for h in range(N_HEADS_GROUP):
        acc = jnp.zeros(ids.shape, F32)
        for k in range(N_BUCKETS):
            acc = jnp.where(ids == k, tab_ref[k, head0 + h], acc)
        o_ref[0, :, h * c:(h + 1) * c] = acc


def _bias_tiles(rel_bias, ids, head0):
    n, r, c = ids.shape
    return pl.pallas_call(
        functools.partial(_bias_tiles_body, head0=head0),
        grid=(n,),
        in_specs=[pl.BlockSpec(memory_space=pltpu.SMEM),
                  pl.BlockSpec((1, r, c), lambda i: (i, 0, 0))],
        out_specs=pl.BlockSpec((1, r, N_HEADS_GROUP * c), lambda i: (i, 0, 0)),
        out_shape=jax.ShapeDtypeStruct((n, r, N_HEADS_GROUP * c), F32),
        compiler_params=_params(("arbitrary",)),
        name="bias_tiles",
    )(rel_bias, jnp.asarray(ids))


FFN_CHUNK = 256
FFN_TOKENS = 1024


def _swiglu_residual(x, gain, wg_ref, wu_ref, wd_ref):
    h = _rms(x, gain).astype(BF16)
    d_ff = wg_ref.shape[1]
    hidden = []
    for c0 in range(0, d_ff, FFN_CHUNK):
        g = _dot(h, wg_ref[:, c0:c0 + FFN_CHUNK])
        u = _dot(h, wu_ref[:, c0:c0 + FFN_CHUNK])
        hidden.append((g * jax.nn.sigmoid(g) * u).astype(BF16))
    return x + 0.5 * _dot(jnp.concatenate(hidden, axis=1), wd_ref[...])


def _ffn_body(x_ref, g_ref, wg_ref, wu_ref, wd_ref, o_ref):
    o_ref[...] = _swiglu_residual(x_ref[...], g_ref[...], wg_ref, wu_ref, wd_ref)


def _ffn(x2d, gain, wg, wu, wd, tm):
    m, d = x2d.shape
    return pl.pallas_call(
        _ffn_body,
        grid=(m // tm,),
        in_specs=[pl.BlockSpec((tm, d), lambda i: (i, 0)),
                  _resident((1, d)), _resident(wg.shape), _resident(wu.shape), _resident(wd.shape)],
        out_specs=pl.BlockSpec((tm, d), lambda i: (i, 0)),
        out_shape=jax.ShapeDtypeStruct((m, d), F32),
        compiler_params=_params(("parallel",)),
        name="ffn",
    )(x2d, gain, wg, wu, wd)


def _inproj_body(x_ref, g_ref, w_ref, za_ref, zb_ref, zc_ref, zd_ref):
    h = _rms(x_ref[...], g_ref[...]).astype(BF16)
    c0 = 0
    for ref in (za_ref, zb_ref, zc_ref, zd_ref):
        w = ref.shape[1]
        ref[...] = _dot(h, w_ref[:, c0:c0 + w])
        c0 += w


def _inproj(x2d, gain, w_packed, tm):
    m, d = x2d.shape
    widths = (GA_W, GB_W, GC_W, GD_W)
    return pl.pallas_call(
        _inproj_body,
        grid=(m // tm,),
        in_specs=[pl.BlockSpec((tm, d), lambda i: (i, 0)), _resident((1, d)), _resident(w_packed.shape)],
        out_specs=[pl.BlockSpec((tm, w), lambda i: (i, 0)) for w in widths],
        out_shape=[jax.ShapeDtypeStruct((m, w), F32) for w in widths],
        compiler_params=_params(("parallel",)),
        name="inproj",
    )(x2d, gain, w_packed)


def _pack_w_in(w):
    d = w.shape[0]
    pad = jnp.zeros((d, GA_W - 680), w.dtype)
    return jnp.concatenate([w[:, 0:256], w[:, 384:640], w[:, 256:384], w[:, 640:680], pad, w[:, 680:]],
                           axis=1).astype(BF16)


def _online_softmax_step(s, vt_bf16, m, l, acc):
    m_new = jnp.maximum(m, jnp.max(s, axis=0, keepdims=True))
    alpha = jnp.exp(m - m_new)
    p = jnp.exp(s - m_new)
    l = alpha * l + jnp.sum(p, axis=0, keepdims=True)
    acc = alpha * acc + _dot(vt_bf16, p.astype(BF16))
    return m_new, l, acc


DSA_TQ = 256
DSA_TK = LANES
DSA_TILE_PAD = 4
DSA_SCORE_GROUP = 4
DSA_COUNT_GROUP = 4
DSA_ATTN_GROUP = 4
DSA_FIELD_BITS = 15


def _dsa_body(q_ref, iq_ref, iwq_ref, ckv_ref, ikw_ref, gain_ref, wuk_ref, wuvt_ref, bias_ref,
              o_ref, c_sc, ct_sc, ik_sc, key_sc, packed_sc, *, topk):
    tq, tk = DSA_TQ, DSA_TK
    n = pl.program_id(1)
    seq = ckv_ref.shape[0]
    nheads = N_HEADS_GROUP
    lane = lax.broadcasted_iota(I32, (1, LANES), 1)

    @pl.when(n == 0)
    def _():
        for r0 in range(0, seq, tk):
            c = _rms(ckv_ref[r0:r0 + tk, :], gain_ref[...])
            c_sc[r0:r0 + tk, :] = c.astype(BF16)
            ct_sc[:, r0:r0 + tk] = c.T.astype(BF16)
            ik = jnp.where(lane < IDX_DIM, ikw_ref[r0:r0 + tk, :], 0.0)
            ik4 = ik
            for rep in range(1, LANES // IDX_DIM):
                ik4 = ik4 + pltpu.roll(ik, rep * IDX_DIM, 1)
            ik_sc[r0:r0 + tk, :] = ik4.astype(BF16)

    iq = iq_ref[...]
    iw_t = (iwq_ref[...] * (IDX_HEADS * IDX_DIM) ** -0.5).T
    tiles_per_q = tq // tk
    heads_per_group = LANES // IDX_DIM
    lhs = []
    for h in range(IDX_HEADS):
        blk = iq[:, LANES * (h // heads_per_group):LANES * (h // heads_per_group + 1)]
        lo = IDX_DIM * (h % heads_per_group)
        lhs.append(jnp.where((lane >= lo) & (lane < lo + IDX_DIM), blk, 0.0).astype(BF16))
    lhs = jnp.concatenate(lhs, axis=0)
    qpos = n * tq + lax.broadcasted_iota(I32, (1, tq), 1)
    srow = lax.broadcasted_iota(I32, (tk, 1), 0)
    pad_shift = DSA_TILE_PAD.bit_length() - 1
    ntiles = lax.shift_left(
        lax.shift_right_logical((n + 1) * tiles_per_q + DSA_TILE_PAD - 1, pad_shift), pad_shift)

    def over_key_tiles(group, body, carry):
        ngroups = lax.shift_right_logical(ntiles, group.bit_length() - 1)
        return lax.fori_loop(0, ngroups, lambda i, c: body(i * group, group, c), carry)

    def score_tiles(kt, g, carry):
        k0 = pl.multiple_of(kt * tk, tk)
        s = _dot_nt(ik_sc[pl.ds(k0, g * tk), :], lhs)
        idx = jnp.zeros((g * tk, tq), F32)
        for h in range(IDX_HEADS):
            idx = idx + iw_t[A_IW_LANE + h:A_IW_LANE + h + 1, :] * jnp.maximum(s[:, h * tq:(h + 1) * tq], 0.0)
        bits = pltpu.bitcast(idx, I32)
        keys = bits ^ ((bits >> 31) & 0x7FFFFFFF)
        spos = k0 + lax.broadcasted_iota(I32, (g * tk, 1), 0)
        keys = jnp.where(spos <= qpos, keys, INT_MIN)
        key_sc[pl.ds(k0, g * tk), :] = keys
        top = lax.shift_right_logical(keys ^ INT_MIN, 32 - DSA_FIELD_BITS)
        for i in range(g):
            packed_sc[0, pl.ds(pl.multiple_of((kt + i) * half, half), half), :] = pack_fields(top[i * tk:(i + 1) * tk])
        return carry

    half = tk // 2
    field_max = (1 << DSA_FIELD_BITS) - 1
    guards = -2147450880

    def pack_fields(f):
        return lax.shift_left(f[:half], 16) | f[half:] | guards

    over_key_tiles(DSA_SCORE_GROUP, score_tiles, 0)

    def count_fields(plane, c):
        cc = lax.shift_left(c, 16) | c

        def body(kt, g, acc):
            for i in range(g):
                r0 = pl.multiple_of((kt + i) * half, half)
                acc = acc + (lax.shift_right_logical(packed_sc[plane, pl.ds(r0, half), :] - cc, 15) & 0x00010001)
            return acc
        tot = jnp.sum(over_key_tiles(DSA_COUNT_GROUP, body, jnp.zeros((half, tq), I32)), axis=0, keepdims=True)
        return (tot & 0xFFFF) + lax.shift_right_logical(tot, 16)

    def field_search(plane):
        def it(i, t):
            cand = t | lax.shift_left(jnp.int32(1), DSA_FIELD_BITS - 1 - i)
            return jnp.where(count_fields(plane, cand) >= topk, cand, t)
        return lax.fori_loop(0, DSA_FIELD_BITS, it, jnp.zeros((1, tq), I32))

    def count(pred):
        def body(kt, g, acc):
            for i in range(g):
                k0 = pl.multiple_of((kt + i) * tk, tk)
                acc = acc + jnp.where(pred(key_sc[pl.ds(k0, tk), :], k0), 1, 0)
            return acc
        acc = over_key_tiles(DSA_COUNT_GROUP, body, jnp.zeros((tk, tq), I32))
        return jnp.sum(acc, axis=0, keepdims=True)

    def bit_iter(i, t):
        cand = t | lax.shift_left(jnp.int32(1), 31 - i)
        cand_s = cand ^ INT_MIN
        cnt = count(lambda keys, k0: keys >= cand_s)
        return jnp.where(cnt >= topk, cand, t)

    t_top = field_search(0)

    def narrow(kt, g, carry):
        for i in range(g):
            k0 = pl.multiple_of((kt + i) * tk, tk)
            u = key_sc[pl.ds(k0, tk), :] ^ INT_MIN
            top = lax.shift_right_logical(u, 32 - DSA_FIELD_BITS)
            nxt = lax.shift_right_logical(u, 32 - 2 * DSA_FIELD_BITS) & field_max
            f = jnp.where(top == t_top, nxt, jnp.where(top > t_top, field_max, 0))
            packed_sc[1, pl.ds(pl.multiple_of((kt + i) * half, half), half), :] = pack_fields(f)
        return carry

    over_key_tiles(DSA_COUNT_GROUP, narrow, 0)
    t_mid = field_search(1)
    t = lax.shift_left(t_top, 32 - DSA_FIELD_BITS) | lax.shift_left(t_mid, 32 - 2 * DSA_FIELD_BITS)
    t = lax.fori_loop(2 * DSA_FIELD_BITS, 32, bit_iter, t)
    tau = t ^ INT_MIN
    cnt_gt = count(lambda keys, k0: keys > tau)
    cnt_eq = count(lambda keys, k0: keys == tau)
    need = topk - cnt_gt
    tie_q = (cnt_eq > need) & (t != 0)

    @pl.when(jnp.max(tie_q.astype(I32)) > 0)
    def _():
        def pos_iter(i, p):
            cand = p | lax.shift_left(jnp.int32(1), (seq.bit_length() - 1) - i)
            cnt = count(lambda keys, k0: (keys == tau) & (k0 + srow < cand))
            return jnp.where(cnt < need, cand, p)
        p = lax.fori_loop(0, seq.bit_length(), pos_iter, jnp.zeros((1, tq), I32))
        p = jnp.where(tie_q, p, seq)

        def drop(kt, carry):
            k0 = pl.multiple_of(kt * tk, tk)
            keys = key_sc[pl.ds(k0, tk), :]
            key_sc[pl.ds(k0, tk), :] = jnp.where((keys == tau) & (k0 + srow > p), INT_MIN, keys)
            return carry
        lax.fori_loop(0, ntiles, drop, 0)

    tau_sel = jnp.maximum(tau, INT_MIN + 1)

    q = q_ref[...]
    qs = []
    for h in range(nheads):
        qh = q[:, h * HEAD_DIM:(h + 1) * HEAD_DIM].astype(BF16)
        qs.append((_dot(qh, wuk_ref[h]) * ATTN_SCALE).astype(BF16))
    qs = jnp.concatenate(qs, axis=0)
    cols = nheads * tq

    def bias_rows(kt):
        parts = [bias_ref[jnp.maximum(n * tiles_per_q + qa - kt, 0)] for qa in range(tiles_per_q)]
        return jnp.concatenate([p[:, h * LANES:(h + 1) * LANES] for h in range(nheads) for p in parts], axis=1)

    def attn_tiles(kt, g, carry):
        m, l, acc = carry
        k0 = pl.multiple_of(kt * tk, tk)
        s = _dot_nt(c_sc[pl.ds(k0, g * tk), :], qs)
        drop = jnp.where(key_sc[pl.ds(k0, g * tk), :] >= tau_sel, 0.0, NEG)
        bias = jnp.concatenate([bias_rows(kt + i) for i in range(g)], axis=0)
        s = s + bias + jnp.concatenate([drop] * nheads, axis=1)
        return _online_softmax_step(s, ct_sc[:, pl.ds(k0, g * tk)], m, l, acc)

    m0 = jnp.full((1, cols), NEG, F32)
    l0 = jnp.zeros((1, cols), F32)
    acc0 = jnp.zeros((KV_RANK, cols), F32)
    m, l, acc = over_key_tiles(DSA_ATTN_GROUP, attn_tiles, (m0, l0, acc0))
    ctx = (acc / l).astype(BF16)
    out_t = jnp.concatenate([_dot(wuvt_ref[h], ctx[:, h * tq:(h + 1) * tq]) for h in range(nheads)], axis=0)
    o_ref[...] = out_t.T


def _dsa(za, ckv_gain, wuk_t, wuv_t, bias, topk):
    b, seq, _ = za.shape
    tq = DSA_TQ
    assert seq % tq == 0 and seq % (DSA_TK * DSA_TILE_PAD) == 0
    return pl.pallas_call(
        functools.partial(_dsa_body, topk=topk),
        grid=(b, seq // tq),
        in_specs=[pl.BlockSpec((None, tq, 256), lambda i, j: (i, j, 0)),
                  pl.BlockSpec((None, tq, 256), lambda i, j: (i, j, 1)),
                  pl.BlockSpec((None, tq, LANES), lambda i, j: (i, j, 5)),
                  pl.BlockSpec((None, seq, LANES), lambda i, j: (i, 0, 4)),
                  pl.BlockSpec((None, seq, LANES), lambda i, j: (i, 0, 5)),
                  _resident(ckv_gain.shape), _resident(wuk_t.shape), _resident(wuv_t.shape),
                  _resident(bias.shape)],
        out_specs=pl.BlockSpec((None, tq, GROUP_WIDTH), lambda i, j: (i, j, 0)),
        out_shape=jax.ShapeDtypeStruct((b, seq, GROUP_WIDTH), F32),
        scratch_shapes=[pltpu.VMEM((seq, KV_RANK), BF16), pltpu.VMEM((KV_RANK, seq), BF16),
                        pltpu.VMEM((seq, LANES), BF16), pltpu.VMEM((seq, tq), I32),
                        pltpu.VMEM((2, seq // 2, tq), I32)],
        compiler_params=_params(("parallel", "arbitrary")),
        name="dsa",
    )(za, za, za, za, za, ckv_gain, wuk_t, wuv_t, bias)


MOBA_GROUP = 2


def _moba_bias_tile(bias_ref, d):
    lo = jnp.maximum(2 * d - 1, 0)
    t_same, t_next, t_prev = bias_ref[2 * d], bias_ref[2 * d + 1], bias_ref[lo]
    top, bot = [], []
    for h in range(N_HEADS_GROUP):
        hs = slice(h * LANES, (h + 1) * LANES)
        top += [t_same[:, hs], t_next[:, hs]]
        bot += [t_prev[:, hs], t_same[:, hs]]
    return jnp.concatenate([jnp.concatenate(top, axis=1), jnp.concatenate(bot, axis=1)], axis=0)


def _moba_body(q_ref, k_ref, v_ref, bias_ref, o_ref, kh_sc, vt_sc, km_sc, sel_sc, *, topb):
    blk = MOBA_BLOCK
    n = pl.program_id(1)
    seq = k_ref.shape[0]
    nb = seq // blk
    nheads = N_HEADS_GROUP

    @pl.when(n == 0)
    def _():
        for j in range(nb):
            kb = k_ref[j * blk:(j + 1) * blk, :]
            vbt = v_ref[j * blk:(j + 1) * blk, :].T
            km = jnp.mean(kb, axis=0, keepdims=True)
            for h in range(nheads):
                sl = slice(h * HEAD_DIM, (h + 1) * HEAD_DIM)
                kh_sc[h, j * blk:(j + 1) * blk, :] = kb[:, sl].astype(BF16)
                vt_sc[h, :, j * blk:(j + 1) * blk] = vbt[sl, :].astype(BF16)
                km_sc[h, j:j + 1, :] = km[:, sl]

    q = q_ref[...]
    rowid = lax.broadcasted_iota(I32, (nb, blk), 0)
    qs = []
    for h in range(nheads):
        qh = q[:, h * HEAD_DIM:(h + 1) * HEAD_DIM]
        km_hi, km_mid, _ = _split3(km_sc[h])
        q_hi, q_mid, _ = _split3(qh)
        gate = _dot_nt(km_hi, q_hi) + _dot_nt(km_mid, q_hi) + _dot_nt(km_hi, q_mid)
        gate = jnp.where(rowid < n, gate, -jnp.inf)
        rank = jnp.zeros((nb, blk), I32)
        for mth in range(nb):
            gm = gate[mth:mth + 1, :]
            beats = (gm > gate) | ((gm == gate) & (mth < rowid))
            rank = rank + jnp.where(beats, 1, 0)
        sel_sc[:, h * blk:(h + 1) * blk] = jnp.where((rank < topb) & (rowid < n), 0.0, NEG)
        qs.append((qh * ATTN_SCALE).astype(BF16))

    def logits(j0, d):
        s = jnp.concatenate([_dot_nt(kh_sc[h, pl.ds(j0, blk), :], qs[h]) for h in range(nheads)], axis=1)
        return s + _moba_bias_tile(bias_ref, d)

    def weighted_values(j0, p):
        p = p.astype(BF16)
        return jnp.concatenate([_dot(vt_sc[h, :, pl.ds(j0, blk)], p[:, h * blk:(h + 1) * blk])
                                for h in range(nheads)], axis=1)

    k0 = pl.multiple_of(n * blk, blk)
    wide = (blk, nheads * blk)
    causal = lax.broadcasted_iota(I32, wide, 0) <= (lax.broadcasted_iota(I32, wide, 1) & (blk - 1))
    s = jnp.where(causal, logits(k0, 0), NEG)
    m = jnp.max(s, axis=0, keepdims=True)
    p = jnp.exp(s - m)
    l = jnp.sum(p, axis=0, keepdims=True)
    acc = weighted_values(k0, p)

    def past_blocks(i, carry):
        m, l, acc = carry
        js = [i * MOBA_GROUP + g for g in range(MOBA_GROUP)]
        j0s = [pl.multiple_of(j * blk, blk) for j in js]
        s = jnp.concatenate([logits(j0, n - j) + sel_sc[pl.ds(j, 1), :] for j, j0 in zip(js, j0s)], axis=0)
        m_new = jnp.maximum(m, jnp.max(s, axis=0, keepdims=True))
        alpha = jnp.exp(m - m_new)
        p = jnp.exp(s - m_new)
        l = alpha * l + jnp.sum(p, axis=0, keepdims=True)
        acc = alpha * acc
        for g, j0 in enumerate(j0s):
            acc = acc + weighted_values(j0, p[g * blk:(g + 1) * blk])
        return m_new, l, acc

    ntrips = lax.shift_right_logical(n + MOBA_GROUP - 1, MOBA_GROUP.bit_length() - 1)
    m, l, acc = lax.fori_loop(0, ntrips, past_blocks, (m, l, acc))
    out = acc / l
    out_t = jnp.concatenate([out[:, h * blk:(h + 1) * blk] for h in range(nheads)], axis=0)
    o_ref[...] = out_t.T


def _moba(zb, bias):
    b, seq, _ = zb.shape
    blk = MOBA_BLOCK
    nb = seq // blk
    return pl.pallas_call(
        functools.partial(_moba_body, topb=min(MOBA_TOPK, nb)),
        grid=(b, nb),
        in_specs=[pl.BlockSpec((None, blk, GROUP_WIDTH), lambda i, j: (i, j, 0)),
                  pl.BlockSpec((None, seq, GROUP_WIDTH), lambda i, j: (i, 0, 1)),
                  pl.BlockSpec((None, seq, GROUP_WIDTH), lambda i, j: (i, 0, 2)),
                  _resident(bias.shape)],
        out_specs=pl.BlockSpec((None, blk, GROUP_WIDTH), lambda i, j: (i, j, 0)),
        out_shape=jax.ShapeDtypeStruct((b, seq, GROUP_WIDTH), F32),
        scratch_shapes=[pltpu.VMEM((N_HEADS_GROUP, seq, HEAD_DIM), BF16),
                        pltpu.VMEM((N_HEADS_GROUP, HEAD_DIM, seq), BF16),
                        pltpu.VMEM((N_HEADS_GROUP, nb, HEAD_DIM), F32),
                        pltpu.VMEM((nb, N_HEADS_GROUP * blk), F32)],
        compiler_params=_params(("parallel", "arbitrary")),
        name="moba",
    )(zb, zb, zb, bias)


HGRN_STEP = 256
_HGRN_SUBS = (16, 4, 1)
_HGRN_FANOUT = 4


def _hgrn_constants():
    c = HGRN_CHUNK
    fan = _HGRN_FANOUT
    t = np.arange(c)[:, None]
    u = np.arange(c)[None, :]
    sel = [(u <= t),
           (u > t),
           np.ones((8, c), bool)]
    masks = []
    for sub in _HGRN_SUBS:
        pos = (t // sub) % fan
        if sub > 1:
            e_own = (t // sub) * sub + sub - 1
            sel.append((u > t) & (u <= e_own))
        for j in range(fan - 1):
            e_j = (t // (fan * sub)) * fan * sub + j * sub + sub - 1
            later = pos > j
            sel.append(later & (u > e_j) & (u <= t))
            s = u
            same_blk = (t // (fan * sub)) == (s // (fan * sub))
            masks.append(same_blk & later & ((s // sub) % fan == j))
    masks.append(t == u)
    sel = np.concatenate(sel, axis=0).astype(np.float32)
    masks = np.stack([np.tile(mk, (HGRN_STEP // c, N_HEADS_GROUP)) for mk in masks]).astype(np.float32)
    hid = np.arange(GROUP_WIDTH) // HEAD_DIM
    blockdiag = (hid[:, None] == hid[None, :]).astype(np.float32)
    return sel, masks, blockdiag


def _hgrn_body(z_ref, lbl_ref, gain_ref, sel_ref, mask_ref, bd_ref, o_ref, st_sc, *, layer):
    c = HGRN_CHUNK
    w = GROUP_WIDTH
    nheads = N_HEADS_GROUP

    @pl.when(pl.program_id(1) == 0)
    def _():
        st_sc[...] = jnp.zeros(st_sc.shape, F32)

    logits = lbl_ref[...]
    e = jnp.exp(logits - jnp.max(logits, axis=0, keepdims=True))
    lb_w = e / jnp.sum(e, axis=0, keepdims=True)
    lb = jnp.sum(lb_w[0:layer + 1], axis=0, keepdims=True) - lb_w[0:1]

    bd = bd_ref[...] > 0.5
    bd_bf16 = bd_ref[...].astype(BF16)
    sel = sel_ref[...].astype(BF16)

    def blockdiag(x_bf16):
        return jnp.where(bd, jnp.concatenate([x_bf16] * nheads, axis=0), jnp.zeros((), BF16))

    chunks = [slice(r0, r0 + c) for r0 in range(0, z_ref.shape[0], c)]

    def per_chunk(fn):
        return jnp.concatenate([fn(rc) for rc in chunks], axis=0)

    q = z_ref[:, 0:w]
    f = lb + (1.0 - lb) * jax.nn.sigmoid(z_ref[:, w:2 * w])
    kin = 1.0 - f
    hi, mid, lo = _split3(jnp.log(f))
    sel3 = jnp.concatenate([sel] * 3, axis=1)
    decays = [jnp.exp(_dot(sel3, jnp.concatenate([hi[rc], mid[rc], lo[rc]], axis=0))) for rc in chunks]

    def decay_rows(base):
        return jnp.concatenate([d[base:base + c] for d in decays], axis=0)

    a = jnp.zeros(q.shape, F32)
    base = 2 * c + 8
    mi = 0
    for sub in _HGRN_SUBS:
        if sub > 1:
            km = (kin * decay_rows(base)).astype(BF16)
            base += c
        else:
            km = kin.astype(BF16)
        variants = [(q * decay_rows(base + j * c)).astype(BF16) for j in range(_HGRN_FANOUT - 1)]
        base += (_HGRN_FANOUT - 1) * c
        if sub == 1:
            variants.append(q.astype(BF16))
        prods = [_dot_nt(jnp.concatenate([v[rc] for v in variants], axis=0), blockdiag(km[rc])) for rc in chunks]
        for j in range(len(variants)):
            a = a + mask_ref[mi] * jnp.concatenate([p[j * c:(j + 1) * c] for p in prods], axis=0)
            mi += 1

    a_bf16 = a.astype(BF16)
    v_bf16 = z_ref[:, 2 * w:3 * w].astype(BF16)
    o = per_chunk(lambda rc: _dot(a_bf16[rc], blockdiag(v_bf16[rc])))
    khat = (kin * decay_rows(c)).astype(BF16)
    deltas = [jnp.where(bd, _dot_tn(v_bf16[rc], khat[rc]), 0.0) for rc in chunks]
    q_dec = (q * decay_rows(0)).astype(BF16)
    st = st_sc[...]
    carried = []
    for ci, rc in enumerate(chunks):
        carried.append(_dot_nt(q_dec[rc], st.astype(BF16)))
        st = st * decays[ci][2 * c:2 * c + 1] + deltas[ci]
    st_sc[...] = st
    o = o + jnp.concatenate(carried, axis=0)

    oo = o * o
    oo_hi = oo.astype(BF16)
    oo_lo = (oo - oo_hi.astype(F32)).astype(BF16)
    ms = (_dot(oo_hi, bd_bf16) + _dot(oo_lo, bd_bf16)) * (1.0 / HEAD_DIM)
    g = z_ref[:, 3 * w:4 * w]
    o_ref[...] = o * lax.rsqrt(ms + EPS) * gain_ref[...] * (g * jax.nn.sigmoid(g))


def _hgrn(zc, lb_logits, gain, layer):
    b, seq, _ = zc.shape
    sel, masks, blockdiag = _hgrn_constants()
    return pl.pallas_call(
        functools.partial(_hgrn_body, layer=layer),
        grid=(b, seq // HGRN_STEP),
        in_specs=[pl.BlockSpec((None, HGRN_STEP, GC_W), lambda i, j: (i, j, 0)),
                  _resident(lb_logits.shape), _resident(gain.shape),
                  _resident(sel.shape), _resident(masks.shape), _resident(blockdiag.shape)],
        out_specs=pl.BlockSpec((None, HGRN_STEP, GROUP_WIDTH), lambda i, j: (i, j, 0)),
        out_shape=jax.ShapeDtypeStruct((b, seq, GROUP_WIDTH), F32),
        scratch_shapes=[pltpu.VMEM((GROUP_WIDTH, GROUP_WIDTH), F32)],
        compiler_params=_params(("parallel", "arbitrary")),
        name="hgrn",
    )(zc, lb_logits, gain, jnp.asarray(sel), jnp.asarray(masks), jnp.asarray(blockdiag))


DIL_GROUP = 4


def _dilated_body(q_ref, k_ref, v_ref, bias_ref, o_ref, m_sc, l_sc, acc_sc):
    span = DIL_SPAN
    seq = q_ref.shape[0]
    lane = lax.broadcasted_iota(I32, (1, LANES), 1)
    head_lanes = [lane < HEAD_DIM, lane >= HEAD_DIM]
    a_i = lax.broadcasted_iota(I32, (span, 2 * span), 0)
    c_i = lax.broadcasted_iota(I32, (span, 2 * span), 1)
    delta = a_i + span - c_i
    window = (delta >= 0) & (delta <= span)
    half_ones = [jnp.where(lane < HEAD_DIM, 1.0, 0.0).astype(BF16) * jnp.ones((2 * span, 1), BF16),
                 jnp.where(lane >= HEAD_DIM, 1.0, 0.0).astype(BF16) * jnp.ones((2 * span, 1), BF16)]
    last = len(DILATED_BRANCHES) - 1

    for bi, (_, r) in enumerate(DILATED_BRANCHES):
        shift = r.bit_length() - 1

        def rows(start, r=r):
            return pl.ds(start, span) if r == 1 else pl.ds(start, span, stride=r)

        def blocks(i, carry, bi=bi, r=r, shift=shift, rows=rows):
            grp = range(DIL_GROUP)
            starts, q, kcat, vcat, mask = [], [], [], [], []
            for g in grp:
                blk = i * DIL_GROUP + g
                j = blk & (r - 1)
                n = lax.shift_right_logical(blk, shift)
                start = j + r * span * n
                prev = jnp.maximum(start - r * span, j)
                starts.append(start)
                q.append(q_ref[rows(start), :] * ATTN_SCALE)
                kcat.append(jnp.concatenate([k_ref[rows(prev), :], k_ref[rows(start), :]], axis=0).astype(BF16))
                vcat.append(jnp.concatenate([v_ref[rows(prev), :], v_ref[rows(start), :]], axis=0))
                mask.append(window & ((c_i >= span) | (n > 0)))
            mask = jnp.concatenate([jnp.where(mk, 0.0, NEG) for mk in mask], axis=0)
            num = [jnp.zeros((span, LANES), F32) for _ in grp]
            den = [jnp.zeros((span, LANES), F32) for _ in grp]
            m_b = jnp.zeros((DIL_GROUP * span, LANES), F32)
            for h in range(2):
                bias = bias_ref[bi, :, h * 2 * span:(h + 1) * 2 * span]
                logits = jnp.concatenate(
                    [_dot_nt(jnp.where(head_lanes[h], q[g], 0.0).astype(BF16), kcat[g]) + bias for g in grp], axis=0)
                logits = logits + mask
                m = jnp.max(logits, axis=1, keepdims=True)
                p = jnp.exp(logits - m).astype(BF16)
                m_b = jnp.where(head_lanes[h], m, m_b)
                for g in grp:
                    pg = p[g * span:(g + 1) * span]
                    num[g] = num[g] + _dot(pg, jnp.where(head_lanes[h], vcat[g], 0.0).astype(BF16))
                    den[g] = den[g] + _dot(pg, half_ones[h])
            num = jnp.concatenate(num, axis=0)
            den = jnp.concatenate(den, axis=0)
            if bi == 0:
                m_new, l_new, acc_new = m_b, den, num
            else:
                m_old = jnp.concatenate([m_sc[rows(st), :] for st in starts], axis=0)
                l_old = jnp.concatenate([l_sc[rows(st), :] for st in starts], axis=0)
                acc_old = jnp.concatenate([acc_sc[rows(st), :] for st in starts], axis=0)
                m_new = jnp.maximum(m_old, m_b)
                w_old = jnp.exp(m_old - m_new)
                w_b = jnp.exp(m_b - m_new)
                l_new = w_old * l_old + w_b * den
                acc_new = w_old * acc_old + w_b * num
            out = acc_new / l_new if bi == last else None
            for g, st in enumerate(starts):
                sl = slice(g * span, (g + 1) * span)
                if bi == last:
                    o_ref[rows(st), :] = out[sl]
                else:
                    m_sc[rows(st), :] = m_new[sl]
                    l_sc[rows(st), :] = l_new[sl]
                    acc_sc[rows(st), :] = acc_new[sl]
            return carry

        lax.fori_loop(0, seq // (span * DIL_GROUP), blocks, 0)


def _dilated(zd, bias):
    b, seq, _ = zd.shape
    pairs = GROUP_WIDTH // LANES
    assert seq % max(w for w, _ in DILATED_BRANCHES) == 0

    def spec(which):
        return pl.BlockSpec((None, seq, LANES), lambda i, j: (i, 0, pairs * which + j))

    nbr, span, bias_w = bias.shape
    return pl.pallas_call(
        _dilated_body,
        grid=(b, pairs),
        in_specs=[spec(0), spec(1), spec(2),
                  pl.BlockSpec((nbr, span, bias_w // pairs), lambda i, j: (0, 0, j))],
        out_specs=pl.BlockSpec((None, seq, LANES), lambda i, j: (i, 0, j)),
        out_shape=jax.ShapeDtypeStruct((b, seq, GROUP_WIDTH), F32),
        scratch_shapes=[pltpu.VMEM((seq, LANES), F32)] * 3,
        compiler_params=_params(("parallel", "parallel")),
        name="dilated",
    )(zd, zd, zd, bias)


def _out_ffn_body(*refs, final):
    x_ref, oa_ref, ob_ref, oc_ref, od_ref, wo_ref, g_ref, wg_ref, wu_ref, wd_ref = refs[0:10]
    gf_ref = refs[10] if final else None
    o_ref = refs[-1]
    cat = jnp.concatenate([oa_ref[...], ob_ref[...], oc_ref[...], od_ref[...]], axis=1).astype(BF16)
    x = x_ref[...] + _dot(cat, wo_ref[...])
    x = _swiglu_residual(x, g_ref[...], wg_ref, wu_ref, wd_ref)
    if final:
        x = _rms(x, gf_ref[...])
    o_ref[...] = x


def _out_ffn(x2d, mixers, wo, gain, wg, wu, wd, gain_final, tm):
    m, d = x2d.shape
    tile = lambda w: pl.BlockSpec((tm, w), lambda i: (i, 0))
    final = gain_final is not None
    args = [x2d, *mixers, wo, gain, wg, wu, wd] + ([gain_final] if final else [])
    in_specs = ([tile(d)] + [tile(GROUP_WIDTH)] * 4
                + [_resident(wo.shape), _resident((1, d)), _resident(wg.shape), _resident(wu.shape),
                   _resident(wd.shape)] + ([_resident((1, d))] if final else []))
    return pl.pallas_call(
        functools.partial(_out_ffn_body, final=final),
        grid=(m // tm,),
        in_specs=in_specs,
        out_specs=tile(d),
        out_shape=jax.ShapeDtypeStruct((m, d), F32),
        compiler_params=_params(("parallel",)),
        name="out_ffn",
    )(*args)


def kernel(x, norm_ffn1, ffn1_gate, ffn1_up, ffn1_down, norm_mix, w_in, ckv_norm, w_kv_up, hgrn_lb_logits,
           hgrn_norm, w_out, norm_ffn2, ffn2_gate, ffn2_up, ffn2_down, rel_bias, norm_final):
    b, seq, d = x.shape
    depth = w_in.shape[0]
    nh = N_HEADS_GROUP
    m = b * seq
    tm = min(512, seq)
    topk = min(TOPK_MAX, seq // 4)

    toe_ids = _toeplitz_bucket_ids(seq)
    bias_a = _bias_tiles(rel_bias, toe_ids, 0)
    bias_b = _bias_tiles(rel_bias, toe_ids, nh)
    bias_d = _bias_tiles(rel_bias, _dilated_bucket_ids(), 2 * nh)

    x2d = x.reshape(m, d)
    for l in range(depth):
        x2d = _ffn(x2d, norm_ffn1[l][None], ffn1_gate[l].astype(BF16), ffn1_up[l].astype(BF16),
                   ffn1_down[l].astype(BF16), min(FFN_TOKENS, m))
        za, zb, zc, zd = _inproj(x2d, norm_mix[l][None], _pack_w_in(w_in[l]), tm)
        za, zb, zc, zd = (z.reshape(b, seq, -1) for z in (za, zb, zc, zd))

        w_up = w_kv_up[l].reshape(KV_RANK, 2, nh, HEAD_DIM)
        wuk_t = jnp.transpose(w_up[:, 0], (1, 2, 0)).astype(BF16)
        wuv_t = jnp.transpose(w_up[:, 1], (1, 2, 0)).astype(BF16)
        mixers = [_dsa(za, ckv_norm[l][None], wuk_t, wuv_t, bias_a, topk), _moba(zb, bias_b),
                  _hgrn(zc, hgrn_lb_logits, hgrn_norm[l][None], l), _dilated(zd, bias_d)]
        mixers = [o.reshape(m, GROUP_WIDTH) for o in mixers]

        gain_final = norm_final[None] if l == depth - 1 else None
        x2d = _out_ffn(x2d, mixers, w_out[l].astype(BF16), norm_ffn2[l][None],
                       ffn2_gate[l].astype(BF16), ffn2_up[l].astype(BF16), ffn2_down[l].astype(BF16),
                       gain_final, tm)
    return x2d.reshape(b, seq, d)
```

```python
import functools
import math

import jax
import jax.numpy as jnp
import numpy as np
from jax import lax
from jax.experimental import pallas as pl
from jax.experimental.pallas import tpu as pltpu

F32 = jnp.float32
BF16 = jnp.bfloat16
I32 = jnp.int32

HEAD_DIM = 64
N_GROUPS = 4
GROUP_WIDTH = 256
N_HEADS_GROUP = GROUP_WIDTH // HEAD_DIM
ATTN_SCALE = HEAD_DIM ** -0.5
KV_RANK = 128
IDX_HEADS = 8
IDX_DIM = 32
TOPK_MAX = 256
MOBA_BLOCK = 256
MOBA_TOPK = 3
HGRN_CHUNK = 64
DILATED_BRANCHES = ((128, 1), (512, 4), (2048, 16))
DIL_SPAN = 128
N_BUCKETS = 32
MAX_DISTANCE = 2048
EPS = 1e-6

LANES = 128
VMEM_LIMIT_BYTES = 56 * 1024 * 1024

NEG = -1e30
INT_MIN = -2 ** 31

GA_W, GB_W, GC_W, GD_W = 768, 768, 1024, 768
A_IW_LANE = 32


def _dot(a, b):
    return jnp.dot(a, b, preferred_element_type=F32)


def _dot_nt(a, b):
    return lax.dot_general(a, b, (((1,), (1,)), ((), ())), preferred_element_type=F32)


def _dot_tn(a, b):
    return lax.dot_general(a, b, (((0,), (0,)), ((), ())), preferred_element_type=F32)


def _rms(x, gain):
    return x * lax.rsqrt(jnp.mean(x * x, axis=-1, keepdims=True) + EPS) * gain


def _split3(x):
    hi = x.astype(BF16)
    r1 = x - hi.astype(F32)
    mid = r1.astype(BF16)
    lo = (r1 - mid.astype(F32)).astype(BF16)
    return hi, mid, lo


def _params(sem):
    return pltpu.CompilerParams(dimension_semantics=sem, vmem_limit_bytes=VMEM_LIMIT_BYTES)


def _resident(shape):
    nd = len(shape)
    return pl.BlockSpec(shape, lambda *_: (0,) * nd, pipeline_mode=pl.Buffered(1))


def _bucket_np(dist):
    max_exact = N_BUCKETS // 2
    n = np.maximum(dist, 0)
    nf = np.maximum(n, 1).astype(np.float64)
    large = max_exact + (np.log(nf / max_exact) / math.log(MAX_DISTANCE / max_exact)
                         * (N_BUCKETS - max_exact)).astype(np.int64)
    large = np.minimum(large, N_BUCKETS - 1)
    return np.where(n < max_exact, n, large).astype(np.int32)


def _toeplitz_bucket_ids(seq):
    nj = seq // LANES
    s = np.arange(LANES)[:, None]
    q = np.arange(LANES)[None, :]
    j = np.arange(nj)[:, None, None]
    return _bucket_np(LANES * j + q - s)


def _dilated_bucket_ids():
    a = np.arange(DIL_SPAN)[:, None]
    c = np.arange(2 * DIL_SPAN)[None, :]
    delta = a + DIL_SPAN - c
    return np.stack([_bucket_np(delta * r) for _, r in DILATED_BRANCHES])


def _bias_tiles_body(tab_ref, ids_ref, o_ref, *, head0):
    ids = ids_ref[0]
    c = ids.shape[1]
    for h in range(N_HEADS_GROUP):
        acc = jnp.zeros(ids.shape, F32)
        for k in range(N_BUCKETS):
            acc = jnp.where(ids == k, tab_ref[k, head0 + h], acc)
        o_ref[0, :, h * c:(h + 1) * c] = acc


def _bias_tiles(rel_bias, ids, head0):
    n, r, c = ids.shape
    return pl.pallas_call(
        functools.partial(_bias_tiles_body, head0=head0),
        grid=(n,),
        in_specs=[pl.BlockSpec(memory_space=pltpu.SMEM),
                  pl.BlockSpec((1, r, c), lambda i: (i, 0, 0))],
        out_specs=pl.BlockSpec((1, r, N_HEADS_GROUP * c), lambda i: (i, 0, 0)),
        out_shape=jax.ShapeDtypeStruct((n, r, N_HEADS_GROUP * c), F32),
        compiler_params=_params(("arbitrary",)),
        name="bias_tiles",
    )(rel_bias, jnp.asarray(ids))


FFN_CHUNK = 256
FFN_TOKENS = 1024


def _swiglu_residual(x, gain, wg_ref, wu_ref, wd_ref):
    h = _rms(x, gain).astype(BF16)
    d_ff = wg_ref.shape[1]
    hidden = []
    for c0 in range(0, d_ff, FFN_CHUNK):
        g = _dot(h, wg_ref[:, c0:c0 + FFN_CHUNK])
        u = _dot(h, wu_ref[:, c0:c0 + FFN_CHUNK])
        hidden.append((g * jax.nn.sigmoid(g) * u).astype(BF16))
    return x + 0.5 * _dot(jnp.concatenate(hidden, axis=1), wd_ref[...])


def _ffn_body(x_ref, g_ref, wg_ref, wu_ref, wd_ref, o_ref):
    o_ref[...] = _swiglu_residual(x_ref[...], g_ref[...], wg_ref, wu_ref, wd_ref)


def _ffn(x2d, gain, wg, wu, wd, tm):
    m, d = x2d.shape
    return pl.pallas_call(
        _ffn_body,
        grid=(m // tm,),
        in_specs=[pl.BlockSpec((tm, d), lambda i: (i, 0)),
                  _resident((1, d)), _resident(wg.shape), _resident(wu.shape), _resident(wd.shape)],
        out_specs=pl.BlockSpec((tm, d), lambda i: (i, 0)),
        out_shape=jax.ShapeDtypeStruct((m, d), F32),
        compiler_params=_params(("parallel",)),
        name="ffn",
    )(x2d, gain, wg, wu, wd)


def _inproj_body(x_ref, g_ref, w_ref, za_ref, zb_ref, zc_ref, zd_ref):
    h = _rms(x_ref[...], g_ref[...]).astype(BF16)
    c0 = 0
    for ref in (za_ref, zb_ref, zc_ref, zd_ref):
        w = ref.shape[1]
        ref[...] = _dot(h, w_ref[:, c0:c0 + w]).astype(ref.dtype)
        c0 += w


def _inproj(x2d, gain, w_packed, tm):
    m, d = x2d.shape
    widths = (GA_W, GB_W, GC_W, GD_W)
    return pl.pallas_call(
        _inproj_body,
        grid=(m // tm,),
        in_specs=[pl.BlockSpec((tm, d), lambda i: (i, 0)), _resident((1, d)), _resident(w_packed.shape)],
        out_specs=[pl.BlockSpec((tm, w), lambda i: (i, 0)) for w in widths],
        out_shape=[jax.ShapeDtypeStruct((m, w), BF16 if i == 1 else F32) for i, w in enumerate(widths)],
        compiler_params=_params(("parallel",)),
        name="inproj",
    )(x2d, gain, w_packed)


def _pack_w_in(w):
    d = w.shape[0]
    pad = jnp.zeros((d, GA_W - 680), w.dtype)
    return jnp.concatenate([w[:, 0:256], w[:, 384:640], w[:, 256:384], w[:, 640:680], pad, w[:, 680:]],
                           axis=1).astype(BF16)


def _online_softmax_step(s, vt_bf16, m, l, acc):
    m_new = jnp.maximum(m, jnp.max(s, axis=0, keepdims=True))
    alpha = jnp.exp(m - m_new)
    p = jnp.exp(s - m_new)
    l = alpha * l + jnp.sum(p, axis=0, keepdims=True)
    acc = alpha * acc + _dot(vt_bf16, p.astype(BF16))
    return m_new, l, acc


DSA_TQ = 256
DSA_TK = LANES
DSA_TILE_PAD = 4
DSA_SCORE_GROUP = 4
DSA_COUNT_GROUP = 4
DSA_ATTN_GROUP = 4
DSA_FIELD_BITS = 15


def _dsa_body(q_ref, iq_ref, iwq_ref, ckv_ref, ikw_ref, gain_ref, wuk_ref, wuvt_ref, bias_ref,
              o_ref, c_sc, ct_sc, ik_sc, key_sc, packed_sc, *, topk):
    tq, tk = DSA_TQ, DSA_TK
    n = pl.program_id(1)
    seq = ckv_ref.shape[0]
    nheads = N_HEADS_GROUP
    lane = lax.broadcasted_iota(I32, (1, LANES), 1)

    @pl.when(n == 0)
    def _():
        for r0 in range(0, seq, tk):
            c = _rms(ckv_ref[r0:r0 + tk, :], gain_ref[...])
            c_sc[r0:r0 + tk, :] = c.astype(BF16)
            ct_sc[:, r0:r0 + tk] = c.T.astype(BF16)
            ik = jnp.where(lane < IDX_DIM, ikw_ref[r0:r0 + tk, :], 0.0)
            ik4 = ik
            for rep in range(1, LANES // IDX_DIM):
                ik4 = ik4 + pltpu.roll(ik, rep * IDX_DIM, 1)
            ik_sc[r0:r0 + tk, :] = ik4.astype(BF16)

    iq = iq_ref[...]
    iw_t = (iwq_ref[...] * (IDX_HEADS * IDX_DIM) ** -0.5).T
    tiles_per_q = tq // tk
    heads_per_group = LANES // IDX_DIM
    lhs = []
    for h in range(IDX_HEADS):
        blk = iq[:, LANES * (h // heads_per_group):LANES * (h // heads_per_group + 1)]
        lo = IDX_DIM * (h % heads_per_group)
        lhs.append(jnp.where((lane >= lo) & (lane < lo + IDX_DIM), blk, 0.0).astype(BF16))
    lhs = jnp.concatenate(lhs, axis=0)
    qpos = n * tq + lax.broadcasted_iota(I32, (1, tq), 1)
    srow = lax.broadcasted_iota(I32, (tk, 1), 0)
    pad_shift = DSA_TILE_PAD.bit_length() - 1
    ntiles = lax.shift_left(
        lax.shift_right_logical((n + 1) * tiles_per_q + DSA_TILE_PAD - 1, pad_shift), pad_shift)

    def over_key_tiles(group, body, carry):
        ngroups = lax.shift_right_logical(ntiles, group.bit_length() - 1)
        return lax.fori_loop(0, ngroups, lambda i, c: body(i * group, group, c), carry)

    def score_tiles(kt, g, carry):
        k0 = pl.multiple_of(kt * tk, tk)
        s = _dot_nt(ik_sc[pl.ds(k0, g * tk), :], lhs)
        idx = jnp.zeros((g * tk, tq), F32)
        for h in range(IDX_HEADS):
            idx = idx + iw_t[A_IW_LANE + h:A_IW_LANE + h + 1, :] * jnp.maximum(s[:, h * tq:(h + 1) * tq], 0.0)
        bits = pltpu.bitcast(idx, I32)
        keys = bits ^ ((bits >> 31) & 0x7FFFFFFF)
        spos = k0 + lax.broadcasted_iota(I32, (g * tk, 1), 0)
        keys = jnp.where(spos <= qpos, keys, INT_MIN)
        key_sc[pl.ds(k0, g * tk), :] = keys
        top = lax.shift_right_logical(keys ^ INT_MIN, 32 - DSA_FIELD_BITS)
        for i in range(g):
            packed_sc[0, pl.ds(pl.multiple_of((kt + i) * half, half), half), :] = pack_fields(top[i * tk:(i + 1) * tk])
        return carry

    half = tk // 2
    field_max = (1 << DSA_FIELD_BITS) - 1
    guards = -2147450880

    def pack_fields(f):
        return lax.shift_left(f[:half], 16) | f[half:] | guards

    over_key_tiles(DSA_SCORE_GROUP, score_tiles, 0)

    def count_fields(plane, c):
        cc = lax.shift_left(c, 16) | c

        def body(kt, g, acc):
            for i in range(g):
                r0 = pl.multiple_of((kt + i) * half, half)
                acc = acc + (lax.shift_right_logical(packed_sc[plane, pl.ds(r0, half), :] - cc, 15) & 0x00010001)
            return acc
        tot = jnp.sum(over_key_tiles(DSA_COUNT_GROUP, body, jnp.zeros((half, tq), I32)), axis=0, keepdims=True)
        return (tot & 0xFFFF) + lax.shift_right_logical(tot, 16)

    def field_search(plane):
        def it(i, t):
            cand = t | lax.shift_left(jnp.int32(1), DSA_FIELD_BITS - 1 - i)
            return jnp.where(count_fields(plane, cand) >= topk, cand, t)
        return lax.fori_loop(0, DSA_FIELD_BITS, it, jnp.zeros((1, tq), I32))

    def count(pred):
        def body(kt, g, acc):
            for i in range(g):
                k0 = pl.multiple_of((kt + i) * tk, tk)
                acc = acc + jnp.where(pred(key_sc[pl.ds(k0, tk), :], k0), 1, 0)
            return acc
        acc = over_key_tiles(DSA_COUNT_GROUP, body, jnp.zeros((tk, tq), I32))
        return jnp.sum(acc, axis=0, keepdims=True)

    def bit_iter(i, t):
        cand = t | lax.shift_left(jnp.int32(1), 31 - i)
        cand_s = cand ^ INT_MIN
        cnt = count(lambda keys, k0: keys >= cand_s)
        return jnp.where(cnt >= topk, cand, t)

    t_top = field_search(0)

    def narrow(kt, g, carry):
        for i in range(g):
            k0 = pl.multiple_of((kt + i) * tk, tk)
            u = key_sc[pl.ds(k0, tk), :] ^ INT_MIN
            top = lax.shift_right_logical(u, 32 - DSA_FIELD_BITS)
            nxt = lax.shift_right_logical(u, 32 - 2 * DSA_FIELD_BITS) & field_max
            f = jnp.where(top == t_top, nxt, jnp.where(top > t_top, field_max, 0))
            packed_sc[1, pl.ds(pl.multiple_of((kt + i) * half, half), half), :] = pack_fields(f)
        return carry

    over_key_tiles(DSA_COUNT_GROUP, narrow, 0)
    t_mid = field_search(1)
    t = lax.shift_left(t_top, 32 - DSA_FIELD_BITS) | lax.shift_left(t_mid, 32 - 2 * DSA_FIELD_BITS)
    t = lax.fori_loop(2 * DSA_FIELD_BITS, 32, bit_iter, t)
    tau = t ^ INT_MIN
    cnt_gt = count(lambda keys, k0: keys > tau)
    cnt_eq = count(lambda keys, k0: keys == tau)
    need = topk - cnt_gt
    tie_q = (cnt_eq > need) & (t != 0)

    @pl.when(jnp.max(tie_q.astype(I32)) > 0)
    def _():
        def pos_iter(i, p):
            cand = p | lax.shift_left(jnp.int32(1), (seq.bit_length() - 1) - i)
            cnt = count(lambda keys, k0: (keys == tau) & (k0 + srow < cand))
            return jnp.where(cnt < need, cand, p)
        p = lax.fori_loop(0, seq.bit_length(), pos_iter, jnp.zeros((1, tq), I32))
        p = jnp.where(tie_q, p, seq)

        def drop(kt, carry):
            k0 = pl.multiple_of(kt * tk, tk)
            keys = key_sc[pl.ds(k0, tk), :]
            key_sc[pl.ds(k0, tk), :] = jnp.where((keys == tau) & (k0 + srow > p), INT_MIN, keys)
            return carry
        lax.fori_loop(0, ntiles, drop, 0)

    tau_sel = jnp.maximum(tau, INT_MIN + 1)

    q = q_ref[...]
    qs = []
    for h in range(nheads):
        qh = q[:, h * HEAD_DIM:(h + 1) * HEAD_DIM].astype(BF16)
        qs.append((_dot(qh, wuk_ref[h]) * ATTN_SCALE).astype(BF16))
    qs = jnp.concatenate(qs, axis=0)
    cols = nheads * tq

    def bias_rows(kt):
        parts = [bias_ref[jnp.maximum(n * tiles_per_q + qa - kt, 0)] for qa in range(tiles_per_q)]
        return jnp.concatenate([p[:, h * LANES:(h + 1) * LANES] for h in range(nheads) for p in parts], axis=1)

    def attn_tiles(kt, g, carry):
        m, l, acc = carry
        k0 = pl.multiple_of(kt * tk, tk)
        s = _dot_nt(c_sc[pl.ds(k0, g * tk), :], qs)
        drop = jnp.where(key_sc[pl.ds(k0, g * tk), :] >= tau_sel, 0.0, NEG)
        bias = jnp.concatenate([bias_rows(kt + i) for i in range(g)], axis=0)
        s = s + bias + jnp.concatenate([drop] * nheads, axis=1)
        return _online_softmax_step(s, ct_sc[:, pl.ds(k0, g * tk)], m, l, acc)

    m0 = jnp.full((1, cols), NEG, F32)
    l0 = jnp.zeros((1, cols), F32)
    acc0 = jnp.zeros((KV_RANK, cols), F32)
    m, l, acc = over_key_tiles(DSA_ATTN_GROUP, attn_tiles, (m0, l0, acc0))
    ctx = (acc / l).astype(BF16)
    out_t = jnp.concatenate([_dot(wuvt_ref[h], ctx[:, h * tq:(h + 1) * tq]) for h in range(nheads)], axis=0)
    o_ref[...] = out_t.T


def _dsa(za, ckv_gain, wuk_t, wuv_t, bias, topk):
    b, seq, _ = za.shape
    tq = DSA_TQ
    assert seq % tq == 0 and seq % (DSA_TK * DSA_TILE_PAD) == 0
    return pl.pallas_call(
        functools.partial(_dsa_body, topk=topk),
        grid=(b, seq // tq),
        in_specs=[pl.BlockSpec((None, tq, 256), lambda i, j: (i, j, 0)),
                  pl.BlockSpec((None, tq, 256), lambda i, j: (i, j, 1)),
                  pl.BlockSpec((None, tq, LANES), lambda i, j: (i, j, 5)),
                  pl.BlockSpec((None, seq, LANES), lambda i, j: (i, 0, 4)),
                  pl.BlockSpec((None, seq, LANES), lambda i, j: (i, 0, 5)),
                  _resident(ckv_gain.shape), _resident(wuk_t.shape), _resident(wuv_t.shape),
                  _resident(bias.shape)],
        out_specs=pl.BlockSpec((None, tq, GROUP_WIDTH), lambda i, j: (i, j, 0)),
        out_shape=jax.ShapeDtypeStruct((b, seq, GROUP_WIDTH), F32),
        scratch_shapes=[pltpu.VMEM((seq, KV_RANK), BF16), pltpu.VMEM((KV_RANK, seq), BF16),
                        pltpu.VMEM((seq, LANES), BF16), pltpu.VMEM((seq, tq), I32),
                        pltpu.VMEM((2, seq // 2, tq), I32)],
        compiler_params=_params(("parallel", "arbitrary")),
        name="dsa",
    )(za, za, za, za, za, ckv_gain, wuk_t, wuv_t, bias)


MOBA_GROUP = 2


def _moba_bias_tile(bias_ref, d):
    lo = jnp.maximum(2 * d - 1, 0)
    t_same, t_next, t_prev = bias_ref[2 * d], bias_ref[2 * d + 1], bias_ref[lo]
    top, bot = [], []
    for h in range(N_HEADS_GROUP):
        hs = slice(h * LANES, (h + 1) * LANES)
        top += [t_same[:, hs], t_next[:, hs]]
        bot += [t_prev[:, hs], t_same[:, hs]]
    return jnp.concatenate([jnp.concatenate(top, axis=1), jnp.concatenate(bot, axis=1)], axis=0)


def _moba_body(q_ref, k_ref, v_ref, bias_ref, o_ref, kh_sc, vt_sc, km_sc, sel_sc, *, topb):
    blk = MOBA_BLOCK
    n = pl.program_id(1)
    seq = k_ref.shape[0]
    nb = seq // blk
    nheads = N_HEADS_GROUP

    @pl.when(n == 0)
    def _():
        for j in range(nb):
            kb = k_ref[j * blk:(j + 1) * blk, :].astype(F32)
            vbt = v_ref[j * blk:(j + 1) * blk, :].astype(F32).T
            km = jnp.mean(kb, axis=0, keepdims=True)
            for h in range(nheads):
                sl = slice(h * HEAD_DIM, (h + 1) * HEAD_DIM)
                kh_sc[h, j * blk:(j + 1) * blk, :] = kb[:, sl].astype(BF16)
                vt_sc[h, :, j * blk:(j + 1) * blk] = vbt[sl, :].astype(BF16)
                km_sc[h, j:j + 1, :] = km[:, sl]

    q = q_ref[...].astype(F32)
    rowid = lax.broadcasted_iota(I32, (nb, blk), 0)
    qs = []
    for h in range(nheads):
        qh = q[:, h * HEAD_DIM:(h + 1) * HEAD_DIM]
        km_hi, km_mid, _ = _split3(km_sc[h])
        q_hi, q_mid, _ = _split3(qh)
        gate = _dot_nt(km_hi, q_hi) + _dot_nt(km_mid, q_hi) + _dot_nt(km_hi, q_mid)
        gate = jnp.where(rowid < n, gate, -jnp.inf)
        rank = jnp.zeros((nb, blk), I32)
        for mth in range(nb):
            gm = gate[mth:mth + 1, :]
            beats = (gm > gate) | ((gm == gate) & (mth < rowid))
            rank = rank + jnp.where(beats, 1, 0)
        sel_sc[:, h * blk:(h + 1) * blk] = jnp.where((rank < topb) & (rowid < n), 0.0, NEG)
        qs.append((qh * ATTN_SCALE).astype(BF16))

    def logits(j0, d):
        s = jnp.concatenate([_dot_nt(kh_sc[h, pl.ds(j0, blk), :], qs[h]) for h in range(nheads)], axis=1)
        return s + _moba_bias_tile(bias_ref, d)

    def weighted_values(j0, p):
        p = p.astype(BF16)
        return jnp.concatenate([_dot(vt_sc[h, :, pl.ds(j0, blk)], p[:, h * blk:(h + 1) * blk])
                                for h in range(nheads)], axis=1)

    k0 = pl.multiple_of(n * blk, blk)
    wide = (blk, nheads * blk)
    causal = lax.broadcasted_iota(I32, wide, 0) <= (lax.broadcasted_iota(I32, wide, 1) & (blk - 1))
    s = jnp.where(causal, logits(k0, 0), NEG)
    m = jnp.max(s, axis=0, keepdims=True)
    p = jnp.exp(s - m)
    l = jnp.sum(p, axis=0, keepdims=True)
    acc = weighted_values(k0, p)

    def past_blocks(i, carry):
        m, l, acc = carry
        js = [i * MOBA_GROUP + g for g in range(MOBA_GROUP)]
        j0s = [pl.multiple_of(j * blk, blk) for j in js]
        s = jnp.concatenate([logits(j0, n - j) + sel_sc[pl.ds(j, 1), :] for j, j0 in zip(js, j0s)], axis=0)
        m_new = jnp.maximum(m, jnp.max(s, axis=0, keepdims=True))
        alpha = jnp.exp(m - m_new)
        p = jnp.exp(s - m_new)
        l = alpha * l + jnp.sum(p, axis=0, keepdims=True)
        acc = alpha * acc
        for g, j0 in enumerate(j0s):
            acc = acc + weighted_values(j0, p[g * blk:(g + 1) * blk])
        return m_new, l, acc

    ntrips = lax.shift_right_logical(n + MOBA_GROUP - 1, MOBA_GROUP.bit_length() - 1)
    m, l, acc = lax.fori_loop(0, ntrips, past_blocks, (m, l, acc))
    out = acc / l
    out_t = jnp.concatenate([out[:, h * blk:(h + 1) * blk] for h in range(nheads)], axis=0)
    o_ref[...] = out_t.T


def _moba(zb, bias):
    b, seq, _ = zb.shape
    blk = MOBA_BLOCK
    nb = seq // blk
    return pl.pallas_call(
        functools.partial(_moba_body, topb=min(MOBA_TOPK, nb)),
        grid=(b, nb),
        in_specs=[pl.BlockSpec((None, blk, GROUP_WIDTH), lambda i, j: (i, j, 0)),
                  pl.BlockSpec((None, seq, GROUP_WIDTH), lambda i, j: (i, 0, 1)),
                  pl.BlockSpec((None, seq, GROUP_WIDTH), lambda i, j: (i, 0, 2)),
                  _resident(bias.shape)],
        out_specs=pl.BlockSpec((None, blk, GROUP_WIDTH), lambda i, j: (i, j, 0)),
        out_shape=jax.ShapeDtypeStruct((b, seq, GROUP_WIDTH), F32),
        scratch_shapes=[pltpu.VMEM((N_HEADS_GROUP, seq, HEAD_DIM), BF16),
                        pltpu.VMEM((N_HEADS_GROUP, HEAD_DIM, seq), BF16),
                        pltpu.VMEM((N_HEADS_GROUP, nb, HEAD_DIM), F32),
                        pltpu.VMEM((nb, N_HEADS_GROUP * blk), F32)],
        compiler_params=_params(("parallel", "arbitrary")),
        name="moba",
    )(zb, zb, zb, bias)


HGRN_STEP = 256
_HGRN_SUBS = (16, 4, 1)
_HGRN_FANOUT = 4


def _hgrn_constants():
    c = HGRN_CHUNK
    fan = _HGRN_FANOUT
    t = np.arange(c)[:, None]
    u = np.arange(c)[None, :]
    sel = [(u <= t),
           (u > t),
           np.ones((8, c), bool)]
    masks = []
    for sub in _HGRN_SUBS:
        pos = (t // sub) % fan
        if sub > 1:
            e_own = (t // sub) * sub + sub - 1
            sel.append((u > t) & (u <= e_own))
        for j in range(fan - 1):
            e_j = (t // (fan * sub)) * fan * sub + j * sub + sub - 1
            later = pos > j
            sel.append(later & (u > e_j) & (u <= t))
            s = u
            same_blk = (t // (fan * sub)) == (s // (fan * sub))
            masks.append(same_blk & later & ((s // sub) % fan == j))
    masks.append(t == u)
    sel = np.concatenate(sel, axis=0).astype(np.float32)
    masks = np.stack([np.tile(mk, (HGRN_STEP // c, N_HEADS_GROUP)) for mk in masks]).astype(np.float32)
    hid = np.arange(GROUP_WIDTH) // HEAD_DIM
    blockdiag = (hid[:, None] == hid[None, :]).astype(np.float32)
    return sel, masks, blockdiag


def _hgrn_body(z_ref, lbl_ref, gain_ref, sel_ref, mask_ref, bd_ref, o_ref, st_sc, *, layer):
    c = HGRN_CHUNK
    w = GROUP_WIDTH
    nheads = N_HEADS_GROUP

    @pl.when(pl.program_id(1) == 0)
    def _():
        st_sc[...] = jnp.zeros(st_sc.shape, F32)

    logits = lbl_ref[...]
    e = jnp.exp(logits - jnp.max(logits, axis=0, keepdims=True))
    lb_w = e / jnp.sum(e, axis=0, keepdims=True)
    lb = jnp.sum(lb_w[0:layer + 1], axis=0, keepdims=True) - lb_w[0:1]

    bd = bd_ref[...] > 0.5
    bd_bf16 = bd_ref[...].astype(BF16)
    sel = sel_ref[...].astype(BF16)

    def blockdiag(x_bf16):
        return jnp.where(bd, jnp.concatenate([x_bf16] * nheads, axis=0), jnp.zeros((), BF16))

    chunks = [slice(r0, r0 + c) for r0 in range(0, z_ref.shape[0], c)]

    def per_chunk(fn):
        return jnp.concatenate([fn(rc) for rc in chunks], axis=0)

    q = z_ref[:, 0:w]
    f = lb + (1.0 - lb) * jax.nn.sigmoid(z_ref[:, w:2 * w])
    kin = 1.0 - f
    hi, mid, lo = _split3(jnp.log(f))
    sel3 = jnp.concatenate([sel] * 3, axis=1)
    decays = [jnp.exp(_dot(sel3, jnp.concatenate([hi[rc], mid[rc], lo[rc]], axis=0))) for rc in chunks]

    def decay_rows(base):
        return jnp.concatenate([d[base:base + c] for d in decays], axis=0)

    a = jnp.zeros(q.shape, F32)
    base = 2 * c + 8
    mi = 0
    for sub in _HGRN_SUBS:
        if sub > 1:
            km = (kin * decay_rows(base)).astype(BF16)
            base += c
        else:
            km = kin.astype(BF16)
        variants = [(q * decay_rows(base + j * c)).astype(BF16) for j in range(_HGRN_FANOUT - 1)]
        base += (_HGRN_FANOUT - 1) * c
        if sub == 1:
            variants.append(q.astype(BF16))
        prods = [_dot_nt(jnp.concatenate([v[rc] for v in variants], axis=0), blockdiag(km[rc])) for rc in chunks]
        for j in range(len(variants)):
            a = a + mask_ref[mi] * jnp.concatenate([p[j * c:(j + 1) * c] for p in prods], axis=0)
            mi += 1

    a_bf16 = a.astype(BF16)
    v_bf16 = z_ref[:, 2 * w:3 * w].astype(BF16)
    o = per_chunk(lambda rc: _dot(a_bf16[rc], blockdiag(v_bf16[rc])))
    khat = (kin * decay_rows(c)).astype(BF16)
    deltas = [jnp.where(bd, _dot_tn(v_bf16[rc], khat[rc]), 0.0) for rc in chunks]
    q_dec = (q * decay_rows(0)).astype(BF16)
    st = st_sc[...]
    carried = []
    for ci, rc in enumerate(chunks):
        carried.append(_dot_nt(q_dec[rc], st.astype(BF16)))
        st = st * decays[ci][2 * c:2 * c + 1] + deltas[ci]
    st_sc[...] = st
    o = o + jnp.concatenate(carried, axis=0)

    oo = o * o
    oo_hi = oo.astype(BF16)
    oo_lo = (oo - oo_hi.astype(F32)).astype(BF16)
    ms = (_dot(oo_hi, bd_bf16) + _dot(oo_lo, bd_bf16)) * (1.0 / HEAD_DIM)
    g = z_ref[:, 3 * w:4 * w]
    o_ref[...] = o * lax.rsqrt(ms + EPS) * gain_ref[...] * (g * jax.nn.sigmoid(g))


def _hgrn(zc, lb_logits, gain, layer):
    b, seq, _ = zc.shape
    sel, masks, blockdiag = _hgrn_constants()
    return pl.pallas_call(
        functools.partial(_hgrn_body, layer=layer),
        grid=(b, seq // HGRN_STEP),
        in_specs=[pl.BlockSpec((None, HGRN_STEP, GC_W), lambda i, j: (i, j, 0)),
                  _resident(lb_logits.shape), _resident(gain.shape),
                  _resident(sel.shape), _resident(masks.shape), _resident(blockdiag.shape)],
        out_specs=pl.BlockSpec((None, HGRN_STEP, GROUP_WIDTH), lambda i, j: (i, j, 0)),
        out_shape=jax.ShapeDtypeStruct((b, seq, GROUP_WIDTH), F32),
        scratch_shapes=[pltpu.VMEM((GROUP_WIDTH, GROUP_WIDTH), F32)],
        compiler_params=_params(("parallel", "arbitrary")),
        name="hgrn",
    )(zc, lb_logits, gain, jnp.asarray(sel), jnp.asarray(masks), jnp.asarray(blockdiag))


DIL_GROUP = 4


def _dilated_body(q_ref, k_ref, v_ref, bias_ref, o_ref, m_sc, l_sc, acc_sc):
    span = DIL_SPAN
    seq = q_ref.shape[0]
    lane = lax.broadcasted_iota(I32, (1, LANES), 1)
    head_lanes = [lane < HEAD_DIM, lane >= HEAD_DIM]
    a_i = lax.broadcasted_iota(I32, (span, 2 * span), 0)
    c_i = lax.broadcasted_iota(I32, (span, 2 * span), 1)
    delta = a_i + span - c_i
    window = (delta >= 0) & (delta <= span)
    half_ones = [jnp.where(lane < HEAD_DIM, 1.0, 0.0).astype(BF16) * jnp.ones((2 * span, 1), BF16),
                 jnp.where(lane >= HEAD_DIM, 1.0, 0.0).astype(BF16) * jnp.ones((2 * span, 1), BF16)]
    last = len(DILATED_BRANCHES) - 1

    for bi, (_, r) in enumerate(DILATED_BRANCHES):
        shift = r.bit_length() - 1

        def rows(start, r=r):
            return pl.ds(start, span) if r == 1 else pl.ds(start, span, stride=r)

        def blocks(i, carry, bi=bi, r=r, shift=shift, rows=rows):
            grp = range(DIL_GROUP)
            starts, q, kcat, vcat, mask = [], [], [], [], []
            for g in grp:
                blk = i * DIL_GROUP + g
                j = blk & (r - 1)
                n = lax.shift_right_logical(blk, shift)
                start = j + r * span * n
                prev = jnp.maximum(start - r * span, j)
                starts.append(start)
                q.append(q_ref[rows(start), :] * ATTN_SCALE)
                kcat.append(jnp.concatenate([k_ref[rows(prev), :], k_ref[rows(start), :]], axis=0).astype(BF16))
                vcat.append(jnp.concatenate([v_ref[rows(prev), :], v_ref[rows(start), :]], axis=0))
                mask.append(window & ((c_i >= span) | (n > 0)))
            mask = jnp.concatenate([jnp.where(mk, 0.0, NEG) for mk in mask], axis=0)
            num = [jnp.zeros((span, LANES), F32) for _ in grp]
            den = [jnp.zeros((span, LANES), F32) for _ in grp]
            m_b = jnp.zeros((DIL_GROUP * span, LANES), F32)
            for h in range(2):
                bias = bias_ref[bi, :, h * 2 * span:(h + 1) * 2 * span]
                logits = jnp.concatenate(
                    [_dot_nt(jnp.where(head_lanes[h], q[g], 0.0).astype(BF16), kcat[g]) + bias for g in grp], axis=0)
                logits = logits + mask
                m = jnp.max(logits, axis=1, keepdims=True)
                p = jnp.exp(logits - m).astype(BF16)
                m_b = jnp.where(head_lanes[h], m, m_b)
                for g in grp:
                    pg = p[g * span:(g + 1) * span]
                    num[g] = num[g] + _dot(pg, jnp.where(head_lanes[h], vcat[g], 0.0).astype(BF16))
                    den[g] = den[g] + _dot(pg, half_ones[h])
            num = jnp.concatenate(num, axis=0)
            den = jnp.concatenate(den, axis=0)
            if bi == 0:
                m_new, l_new, acc_new = m_b, den, num
            else:
                m_old = jnp.concatenate([m_sc[rows(st), :] for st in starts], axis=0)
                l_old = jnp.concatenate([l_sc[rows(st), :] for st in starts], axis=0)
                acc_old = jnp.concatenate([acc_sc[rows(st), :] for st in starts], axis=0)
                m_new = jnp.maximum(m_old, m_b)
                w_old = jnp.exp(m_old - m_new)
                w_b = jnp.exp(m_b - m_new)
                l_new = w_old * l_old + w_b * den
                acc_new = w_old * acc_old + w_b * num
            out = acc_new / l_new if bi == last else None
            for g, st in enumerate(starts):
                sl = slice(g * span, (g + 1) * span)
                if bi == last:
                    o_ref[rows(st), :] = out[sl]
                else:
                    m_sc[rows(st), :] = m_new[sl]
                    l_sc[rows(st), :] = l_new[sl]
                    acc_sc[rows(st), :] = acc_new[sl]
            return carry

        lax.fori_loop(0, seq // (span * DIL_GROUP), blocks, 0)


def _dilated(zd, bias):
    b, seq, _ = zd.shape
    pairs = GROUP_WIDTH // LANES
    assert seq % max(w for w, _ in DILATED_BRANCHES) == 0

    def spec(which):
        return pl.BlockSpec((None, seq, LANES), lambda i, j: (i, 0, pairs * which + j))

    nbr, span, bias_w = bias.shape
    return pl.pallas_call(
        _dilated_body,
        grid=(b, pairs),
        in_specs=[spec(0), spec(1), spec(2),
                  pl.BlockSpec((nbr, span, bias_w // pairs), lambda i, j: (0, 0, j))],
        out_specs=pl.BlockSpec((None, seq, LANES), lambda i, j: (i, 0, j)),
        out_shape=jax.ShapeDtypeStruct((b, seq, GROUP_WIDTH), F32),
        scratch_shapes=[pltpu.VMEM((seq, LANES), F32)] * 3,
        compiler_params=_params(("parallel", "parallel")),
        name="dilated",
    )(zd, zd, zd, bias)


def _out_ffn_body(*refs, final):
    x_ref, oa_ref, ob_ref, oc_ref, od_ref, wo_ref, g_ref, wg_ref, wu_ref, wd_ref = refs[0:10]
    gf_ref = refs[10] if final else None
    o_ref = refs[-1]
    cat = jnp.concatenate([oa_ref[...], ob_ref[...], oc_ref[...], od_ref[...]], axis=1).astype(BF16)
    x = x_ref[...] + _dot(cat, wo_ref[...])
    x = _swiglu_residual(x, g_ref[...], wg_ref, wu_ref, wd_ref)
    if final:
        x = _rms(x, gf_ref[...])
    o_ref[...] = x


def _out_ffn(x2d, mixers, wo, gain, wg, wu, wd, gain_final, tm):
    m, d = x2d.shape
    tile = lambda w: pl.BlockSpec((tm, w), lambda i: (i, 0))
    final = gain_final is not None
    args = [x2d, *mixers, wo, gain, wg, wu, wd] + ([gain_final] if final else [])
    in_specs = ([tile(d)] + [tile(GROUP_WIDTH)] * 4
                + [_resident(wo.shape), _resident((1, d)), _resident(wg.shape), _resident(wu.shape),
                   _resident(wd.shape)] + ([_resident((1, d))] if final else []))
    return pl.pallas_call(
        functools.partial(_out_ffn_body, final=final),
        grid=(m // tm,),
        in_specs=in_specs,
        out_specs=tile(d),
        out_shape=jax.ShapeDtypeStruct((m, d), F32),
        compiler_params=_params(("parallel",)),
        name="out_ffn",
    )(*args)


def kernel(x, norm_ffn1, ffn1_gate, ffn1_up, ffn1_down, norm_mix, w_in, ckv_norm, w_kv_up, hgrn_lb_logits,
           hgrn_norm, w_out, norm_ffn2, ffn2_gate, ffn2_up, ffn2_down, rel_bias, norm_final):
    b, seq, d = x.shape
    depth = w_in.shape[0]
    nh = N_HEADS_GROUP
    m = b * seq
    tm = min(512, seq)
    topk = min(TOPK_MAX, seq // 4)

    toe_ids = _toeplitz_bucket_ids(seq)
    bias_a = _bias_tiles(rel_bias, toe_ids, 0)
    bias_b = _bias_tiles(rel_bias, toe_ids, nh)
    bias_d = _bias_tiles(rel_bias, _dilated_bucket_ids(), 2 * nh)

    x2d = x.reshape(m, d)
    for l in range(depth):
        x2d = _ffn(x2d, norm_ffn1[l][None], ffn1_gate[l].astype(BF16), ffn1_up[l].astype(BF16),
                   ffn1_down[l].astype(BF16), min(FFN_TOKENS, m))
        za, zb, zc, zd = _inproj(x2d, norm_mix[l][None], _pack_w_in(w_in[l]), tm)
        za, zb, zc, zd = (z.reshape(b, seq, -1) for z in (za, zb, zc, zd))

        w_up = w_kv_up[l].reshape(KV_RANK, 2, nh, HEAD_DIM)
        wuk_t = jnp.transpose(w_up[:, 0], (1, 2, 0)).astype(BF16)
        wuv_t = jnp.transpose(w_up[:, 1], (1, 2, 0)).astype(BF16)
        mixers = [_dsa(za, ckv_norm[l][None], wuk_t, wuv_t, bias_a, topk), _moba(zb, bias_b),
                  _hgrn(zc, hgrn_lb_logits, hgrn_norm[l][None], l), _dilated(zd, bias_d)]
        mixers = [o.reshape(m, GROUP_WIDTH) for o in mixers]

        gain_final = norm_final[None] if l == depth - 1 else None
        x2d = _out_ffn(x2d, mixers, w_out[l].astype(BF16), norm_ffn2[l][None],
                       ffn2_gate[l].astype(BF16), ffn2_up[l].astype(BF16), ffn2_down[l].astype(BF16),
                       gain_final, tm)
    return x2d.reshape(b, seq, d)
```
